```python
import math
import jax
import jax.numpy as jnp
from jax import lax
import numpy as np

D_MODEL = 2048
BATCH = 2
SEQ = 4096
DEPTH = 2
DEC_BATCH = 32
DEC_SEQ = 4
PAST_LEN = 8192
PAGE_SIZE = 128

HEAD_DIM = 128
FOX_HEADS = 8
GDN_HEADS = 8
RET_HEADS = 8
MIX_W = 8 * HEAD_DIM
N_BRANCH = 3
CONV_W = 4
CHUNK = 64
Q_BLOCK = 128
ROPE_BASE = 10000.0
N_EXPERTS = 32
TOP_K = 4
D_FF = D_MODEL
SWIGLU_LIMIT = 7.0
SWIGLU_ALPHA = 1.702
EPS = 1e-6
FOX_BIAS_LO = 3.0
FOX_BIAS_HI = 10.0

IN_SPLITS = (
    ('fox_q', FOX_HEADS * HEAD_DIM),
    ('fox_k', FOX_HEADS * HEAD_DIM),
    ('fox_v', FOX_HEADS * HEAD_DIM),
    ('fox_f', FOX_HEADS),
    ('gdn_qkv', 3 * GDN_HEADS * HEAD_DIM),
    ('gdn_a', GDN_HEADS),
    ('gdn_b', GDN_HEADS),
    ('gdn_z', GDN_HEADS * HEAD_DIM),
    ('ret_q', RET_HEADS * HEAD_DIM),
    ('ret_k', RET_HEADS * HEAD_DIM),
    ('ret_v', RET_HEADS * HEAD_DIM),
    ('ret_g', RET_HEADS * HEAD_DIM),
    ('gate', N_BRANCH * D_MODEL),
)
D_IN = sum(w for _, w in IN_SPLITS)

kernel_name = 'hybrid_fox_gdn_retention_moe_adaln_step'


def rms_norm(x, w=None):
    xf = x.astype(jnp.float32)
    y = xf * lax.rsqrt(jnp.mean(xf * xf, axis=-1, keepdims=True) + EPS)
    if w is not None:
        y = y * w.astype(jnp.float32)
    return y.astype(x.dtype)


def l2_norm(x):
    xf = x.astype(jnp.float32)
    return (xf * lax.rsqrt(jnp.sum(xf * xf, axis=-1, keepdims=True) + EPS)).astype(x.dtype)


def split_columns(z):
    out, off = {}, 0
    for name, width in IN_SPLITS:
        out[name] = z[..., off:off + width]
        off += width
    return out


def heads(x, n):
    return x.reshape(*x.shape[:-1], n, -1)


def rotary(x, pos):
    half = x.shape[-1] // 2
    inv = ROPE_BASE ** (-jnp.arange(half, dtype=jnp.float32) / half)
    ang = pos.astype(jnp.float32)[:, None] * inv[None, :]
    cos = jnp.cos(ang)[None, :, None, :]
    sin = jnp.sin(ang)[None, :, None, :]
    xf = x.astype(jnp.float32)
    x1, x2 = xf[..., :half], xf[..., half:]
    return jnp.concatenate([x1 * cos - x2 * sin, x1 * sin + x2 * cos], axis=-1).astype(x.dtype)


def causal_short_conv(u, buf, w):
    L = u.shape[1]
    full = jnp.concatenate([buf.astype(u.dtype), u], axis=1)
    out = full[:, 0:L] * w[0]
    for i in range(1, CONV_W):
        out = out + full[:, i:i + L] * w[i]
    return jax.nn.silu(out), full[:, L:]


def to_chunks(x, c):
    b, l, h = x.shape[:3]
    x = x.reshape(b, l // c, c, h, *x.shape[3:])
    return jnp.moveaxis(x, (1, 3), (0, 2))


def from_chunks(x):
    x = jnp.moveaxis(x, (0, 2), (1, 3))
    b, n, c, h = x.shape[:4]
    return x.reshape(b, n * c, h, *x.shape[4:])


def fox_attention_prompt(q, k, v, lf):
    b, s, h, d = q.shape
    scale = d ** -0.5
    cum = jnp.cumsum(lf, axis=1).transpose(0, 2, 1)
    nb = s // Q_BLOCK
    qb = q.reshape(b, nb, Q_BLOCK, h, d).transpose(1, 0, 2, 3, 4)
    cb = cum.reshape(b, h, nb, Q_BLOCK).transpose(2, 0, 1, 3)
    pb = jnp.arange(s).reshape(nb, Q_BLOCK)
    kpos = jnp.arange(s)

    def block(args):
        qi, ci, pi = args
        sc = jnp.einsum('bqhd,bkhd->bhqk', qi, k).astype(jnp.float32) * scale
        sc = sc + ci[..., :, None] - cum[:, :, None, :]
        sc = jnp.where(pi[:, None] >= kpos[None, :], sc, -jnp.inf)
        pr = jax.nn.softmax(sc, axis=-1).astype(v.dtype)
        return jnp.einsum('bhqk,bkhd->bqhd', pr, v)

    o = lax.map(block, (qb, cb, pb))
    return o.transpose(1, 0, 2, 3, 4).reshape(b, s, h, d)


def fox_attention_sample(q, k, v, lf, k_past, v_past, lf_past):
    t_new = q.shape[1]
    scale = q.shape[-1] ** -0.5
    c_new = jnp.cumsum(lf, axis=1).transpose(0, 2, 1)
    lfp = lf_past.astype(jnp.float32)
    suffix = (lax.cumsum(lfp, axis=1, reverse=True) - lfp).transpose(0, 2, 1)
    s_past = jnp.einsum('bqhd,bkhd->bhqk', q, k_past.astype(q.dtype)).astype(jnp.float32) * scale
    s_past = s_past + c_new[..., :, None] + suffix[..., None, :]
    s_new = jnp.einsum('bqhd,bkhd->bhqk', q, k).astype(jnp.float32) * scale
    s_new = s_new + c_new[..., :, None] - c_new[..., None, :]
    s_new = jnp.where(jnp.tril(jnp.ones((t_new, t_new), bool)), s_new, -jnp.inf)
    pr = jax.nn.softmax(jnp.concatenate([s_past, s_new], axis=-1), axis=-1).astype(v.dtype)
    n_past = k_past.shape[1]
    return (jnp.einsum('bhqk,bkhd->bqhd', pr[..., :n_past], v_past.astype(v.dtype))
            + jnp.einsum('bhqk,bkhd->bqhd', pr[..., n_past:], v))


def gated_delta_chunked(q, k, v, g, beta, s0):
    f32 = jnp.float32
    L, dk = q.shape[1], q.shape[-1]
    c = math.gcd(L, CHUNK)
    qc = to_chunks(q.astype(f32) * dk ** -0.5, c)
    kc = to_chunks(k.astype(f32), c)
    vc = to_chunks(v.astype(f32), c)
    bc = to_chunks(beta.astype(f32), c)
    gc = jnp.cumsum(to_chunks(g.astype(f32), c), axis=-1)
    incl = jnp.tril(jnp.ones((c, c), bool))
    strict = jnp.tril(jnp.ones((c, c), bool), -1)
    decay = jnp.exp(jnp.where(incl, gc[..., :, None] - gc[..., None, :], -jnp.inf))
    kb = kc * bc[..., None]
    m = jnp.where(strict, jnp.einsum('nbhid,nbhjd->nbhij', kb, kc) * decay, 0.0)
    eye = jnp.eye(c, dtype=f32)
    t = lax.linalg.triangular_solve(eye + m, jnp.broadcast_to(eye, m.shape),
                                    left_side=True, lower=True, unit_diagonal=True)
    u0 = t @ (vc * bc[..., None])
    wk = t @ (kb * jnp.exp(gc)[..., None])
    a_qk = jnp.einsum('nbhid,nbhjd->nbhij', qc, kc) * decay

    def step(s, xs):
        qi, ki, gi, ui, wi, ai = xs
        ui = ui - wi @ s
        oi = (qi * jnp.exp(gi)[..., None]) @ s + ai @ ui
        gl = gi[..., -1:]
        s = s * jnp.exp(gl)[..., None] + jnp.einsum('bhcd,bhce->bhde', ki * jnp.exp(gl - gi)[..., None], ui)
        return s, oi

    s, o = lax.scan(step, s0.astype(f32), (qc, kc, gc, u0, wk, a_qk))
    return from_chunks(o), s


def retention_chunked(q, k, v, s0):
    f32 = jnp.float32
    L, dk = q.shape[1], q.shape[-1]
    c = math.gcd(L, CHUNK)
    lg = jnp.log(1.0 - 2.0 ** (-5.0 - jnp.arange(q.shape[2], dtype=f32)))
    idx = jnp.arange(c, dtype=f32)
    incl = jnp.tril(jnp.ones((c, c), bool))
    decay = jnp.exp(jnp.where(incl, (idx[:, None] - idx[None, :]) * lg[:, None, None], -jnp.inf))
    q_dec = jnp.exp((idx + 1.0) * lg[:, None])[..., None]
    k_dec = jnp.exp((c - 1.0 - idx) * lg[:, None])[..., None]
    c_dec = jnp.exp(c * lg)[:, None, None]
    qc = to_chunks(q.astype(f32), c)
    kc = to_chunks(k.astype(f32) * dk ** -0.5, c)
    vc = to_chunks(v.astype(f32), c)
    a = jnp.einsum('nbhid,nbhjd->nbhij', qc, kc) * decay

    def step(s, xs):
        qi, ki, vi, ai = xs
        oi = ai @ vi + (qi * q_dec) @ s
        s = s * c_dec + jnp.einsum('bhcd,bhce->bhde', ki * k_dec, vi)
        return s, oi

    s, o = lax.scan(step, s0.astype(f32), (qc, kc, vc, a))
    return from_chunks(o), s


def token_mixer(h, pos, p, past):
    b, l, d = h.shape
    z = split_columns(h @ p['w_in'])
    fq = rms_norm(heads(z['fox_q'], FOX_HEADS), p['fox_qn'])
    fk = rms_norm(heads(z['fox_k'], FOX_HEADS), p['fox_kn'])
    fv = heads(z['fox_v'], FOX_HEADS)
    lf = jax.nn.log_sigmoid((z['fox_f'] + p['fox_fb']).astype(jnp.float32))
    if past is None:
        o_fox = fox_attention_prompt(fq, fk, fv, lf)
        conv_buf = jnp.zeros((b, CONV_W - 1, 3 * MIX_W), h.dtype)
        s_gdn0 = jnp.zeros((b, GDN_HEADS, HEAD_DIM, HEAD_DIM), jnp.float32)
        s_ret0 = jnp.zeros((b, RET_HEADS, HEAD_DIM, HEAD_DIM), jnp.float32)
    else:
        o_fox = fox_attention_sample(fq, fk, fv, lf, past['k'], past['v'], past['lf'])
        conv_buf, s_gdn0, s_ret0 = past['conv'], past['gdn'], past['ret']
    qkv, new_conv = causal_short_conv(z['gdn_qkv'], conv_buf, p['gdn_conv'])
    gq, gk, gv = jnp.split(qkv, 3, axis=-1)
    gq = l2_norm(heads(gq, GDN_HEADS))
    gk = l2_norm(heads(gk, GDN_HEADS))
    gv = heads(gv, GDN_HEADS)
    beta = jax.nn.sigmoid(z['gdn_b'].astype(jnp.float32))
    g = -jnp.exp(p['gdn_alog'].astype(jnp.float32)) * jax.nn.softplus(z['gdn_a'].astype(jnp.float32) + p['gdn_dtb'])
    o_gdn, s_gdn = gated_delta_chunked(gq, gk, gv, g, beta, s_gdn0)
    o_gdn = rms_norm(o_gdn.astype(h.dtype), p['gdn_nw']) * jax.nn.silu(heads(z['gdn_z'], GDN_HEADS))
    rq = rotary(heads(z['ret_q'], RET_HEADS), pos)
    rk = rotary(heads(z['ret_k'], RET_HEADS), pos)
    o_ret, s_ret = retention_chunked(rq, rk, heads(z['ret_v'], RET_HEADS), s_ret0)
    o_ret = rms_norm(o_ret.astype(h.dtype)) * jax.nn.silu(heads(z['ret_g'], RET_HEADS))
    br = jnp.stack([o_fox.reshape(b, l, MIX_W), o_gdn.reshape(b, l, MIX_W), o_ret.reshape(b, l, MIX_W)], axis=2)
    proj = jnp.einsum('blnm,nmd->blnd', br, p['w_branch'])
    gates = jax.nn.sigmoid(z['gate'].reshape(b, l, N_BRANCH, d))
    y = jnp.sum(gates * proj, axis=2) @ p['w_out']
    return y, (fk, fv, lf, s_gdn, new_conv, s_ret)


def moe_ffn(h, p):
    shp = h.shape
    x = h.reshape(-1, shp[-1])
    n, d = x.shape
    logits = (x @ p['w_router'] + p['b_router']).astype(jnp.float32)
    top_val, top_idx = lax.top_k(logits, TOP_K)
    top_w = jax.nn.softmax(top_val, axis=-1)
    n_assign = n * TOP_K
    blk = max(8, min(128, n_assign // N_EXPERTS))
    n_blk = -(-n_assign // blk) + N_EXPERTS
    flat_e = top_idx.reshape(-1)
    flat_t = jnp.repeat(jnp.arange(n, dtype=jnp.int32), TOP_K)
    flat_w = top_w.reshape(-1)
    order = jnp.argsort(flat_e)
    e_sorted = flat_e[order]
    counts = jnp.bincount(flat_e, length=N_EXPERTS)
    start = jnp.cumsum(counts) - counts
    padded = (counts + blk - 1) // blk * blk
    pad_end = jnp.cumsum(padded)
    pad_start = pad_end - padded
    dest = pad_start[e_sorted] + jnp.arange(n_assign) - start[e_sorted]
    slot_tok = jnp.full((n_blk * blk,), n, jnp.int32).at[dest].set(flat_t[order])
    slot_w = jnp.zeros((n_blk * blk,), jnp.float32).at[dest].set(flat_w[order])
    blk_e = jnp.minimum(jnp.searchsorted(pad_end, jnp.arange(n_blk) * blk, side='right'), N_EXPERTS - 1)
    xs = jnp.concatenate([x, jnp.zeros((1, d), x.dtype)], axis=0)[slot_tok].reshape(n_blk, blk, d)

    def expert_block(args):
        xb, e = args
        gu = xb @ p['w_e1'][e] + p['b_e1'][e]
        x_glu = jnp.minimum(gu[:, :D_FF], SWIGLU_LIMIT)
        x_lin = jnp.clip(gu[:, D_FF:], -SWIGLU_LIMIT, SWIGLU_LIMIT)
        act = x_glu * jax.nn.sigmoid(SWIGLU_ALPHA * x_glu) * (x_lin + 1.0)
        return act @ p['w_e2'][e] + p['b_e2'][e]

    ys = lax.map(expert_block, (xs, blk_e)).reshape(n_blk * blk, d)
    y = jax.ops.segment_sum(ys * slot_w[:, None].astype(ys.dtype), slot_tok, num_segments=n + 1)[:n]
    return y.reshape(shp)


def decoder_layer(x, c, pos, p, past):
    mod = jax.nn.silu(c) @ p['w_ada'] + p['b_ada']
    sh1, sc1, g1, sh2, sc2, g2 = jnp.split(mod[:, None, :], 6, axis=-1)
    h = rms_norm(x, p['norm1']) * (1.0 + sc1) + sh1
    y, state = token_mixer(h, pos, p, past)
    x = x + g1 * y
    h = rms_norm(x, p['norm2']) * (1.0 + sc2) + sh2
    x = x + g2 * moe_ffn(h, p)
    return x, state


def gather_pages(pool, page_table):
    g = pool[page_table]
    return g.reshape(g.shape[0], g.shape[1] * g.shape[2], *g.shape[3:])


def setup_inputs(seed: int = 0) -> dict:
    key = jax.random.key(seed)
    ks = jax.random.split(key, 32)
    f32 = jnp.float32

    def nrm(i, shape, scale):
        return jax.random.normal(ks[i], shape, f32) * scale

    d = D_MODEL
    n_pages = PAST_LEN // PAGE_SIZE
    n_used = DEC_BATCH * n_pages
    n_pool = n_used + max(1, n_used // 4)
    page_table = jax.random.permutation(ks[0], n_pool)[:n_used].reshape(DEC_BATCH, n_pages).astype(jnp.int32)
    dt = jax.random.uniform(ks[21], (DEPTH, GDN_HEADS), f32, 1e-3, 0.1)
    head_bias = jnp.linspace(FOX_BIAS_LO, FOX_BIAS_HI, FOX_HEADS, dtype=f32)
    return {
        'x_prompt': nrm(1, (BATCH, SEQ, d), 1.0),
        'x_sample': nrm(2, (DEC_BATCH, DEC_SEQ, d), 1.0),
        'cache_fox_k': nrm(3, (DEPTH, n_pool, PAGE_SIZE, FOX_HEADS, HEAD_DIM), 1.0),
        'cache_fox_v': nrm(4, (DEPTH, n_pool, PAGE_SIZE, FOX_HEADS, HEAD_DIM), 1.0),
        'cache_fox_logf': jax.nn.log_sigmoid(head_bias + nrm(5, (DEPTH, n_pool, PAGE_SIZE, FOX_HEADS), 0.5)),
        'state_gdn': nrm(6, (DEPTH, DEC_BATCH, GDN_HEADS, HEAD_DIM, HEAD_DIM), 0.1),
        'state_gdn_conv': nrm(7, (DEPTH, DEC_BATCH, CONV_W - 1, 3 * MIX_W), 1.0),
        'state_ret': nrm(8, (DEPTH, DEC_BATCH, RET_HEADS, HEAD_DIM, HEAD_DIM), 0.1),
        'page_table': page_table,
        'c_prompt': nrm(9, (BATCH, d), 1.0),
        'c_sample': nrm(10, (DEC_BATCH, d), 1.0),
        'norm1_w': 1.0 + nrm(11, (DEPTH, d), 0.05),
        'norm2_w': 1.0 + nrm(12, (DEPTH, d), 0.05),
        'w_ada': nrm(13, (DEPTH, d, 6 * d), 0.5 * d ** -0.5),
        'b_ada': nrm(14, (DEPTH, 6 * d), 0.01),
        'w_in': nrm(15, (DEPTH, d, D_IN), d ** -0.5),
        'fox_f_bias': head_bias + nrm(16, (DEPTH, FOX_HEADS), 0.5),
        'fox_qn_w': 1.0 + nrm(17, (DEPTH, HEAD_DIM), 0.05),
        'fox_kn_w': 1.0 + nrm(18, (DEPTH, HEAD_DIM), 0.05),
        'gdn_conv_w': nrm(19, (DEPTH, CONV_W, 3 * MIX_W), CONV_W ** -0.5),
        'gdn_a_log': jnp.log(jax.random.uniform(ks[20], (DEPTH, GDN_HEADS), f32, 1.0, 16.0)),
        'gdn_dt_bias': dt + jnp.log(-jnp.expm1(-dt)),
        'gdn_norm_w': 1.0 + nrm(22, (DEPTH, HEAD_DIM), 0.05),
        'w_branch': nrm(23, (DEPTH, N_BRANCH, MIX_W, d), MIX_W ** -0.5),
        'w_out': nrm(24, (DEPTH, d, d), d ** -0.5),
        'w_router': nrm(25, (DEPTH, d, N_EXPERTS), d ** -0.5),
        'b_router': nrm(26, (DEPTH, N_EXPERTS), 0.01),
        'w_e1': nrm(27, (DEPTH, N_EXPERTS, d, 2 * D_FF), d ** -0.5),
        'b_e1': nrm(28, (DEPTH, N_EXPERTS, 2 * D_FF), 0.01),
        'w_e2': nrm(29, (DEPTH, N_EXPERTS, D_FF, d), D_FF ** -0.5),
        'b_e2': nrm(30, (DEPTH, N_EXPERTS, d), 0.01),
    }


def reference(x_prompt, x_sample, cache_fox_k, cache_fox_v, cache_fox_logf, state_gdn, state_gdn_conv, state_ret,
              page_table, c_prompt, c_sample, norm1_w, norm2_w, w_ada, b_ada, w_in, fox_f_bias, fox_qn_w, fox_kn_w,
              gdn_conv_w, gdn_a_log, gdn_dt_bias, gdn_norm_w, w_branch, w_out, w_router, b_router,
              w_e1, b_e1, w_e2, b_e2):
    n_pages = page_table.shape[1]
    past_len = n_pages * PAGE_SIZE
    pos_p = jnp.arange(x_prompt.shape[1], dtype=jnp.int32)
    pos_s = past_len + jnp.arange(x_sample.shape[1], dtype=jnp.int32)
    xp, xs = x_prompt, x_sample
    st_p, st_s = [], []
    for l in range(DEPTH):
        p = {'norm1': norm1_w[l], 'norm2': norm2_w[l], 'w_ada': w_ada[l], 'b_ada': b_ada[l], 'w_in': w_in[l],
             'fox_fb': fox_f_bias[l], 'fox_qn': fox_qn_w[l], 'fox_kn': fox_kn_w[l], 'gdn_conv': gdn_conv_w[l],
             'gdn_alog': gdn_a_log[l], 'gdn_dtb': gdn_dt_bias[l], 'gdn_nw': gdn_norm_w[l],
             'w_branch': w_branch[l], 'w_out': w_out[l], 'w_router': w_router[l], 'b_router': b_router[l],
             'w_e1': w_e1[l], 'b_e1': b_e1[l], 'w_e2': w_e2[l], 'b_e2': b_e2[l]}
        past = {'k': gather_pages(cache_fox_k[l], page_table), 'v': gather_pages(cache_fox_v[l], page_table),
                'lf': gather_pages(cache_fox_logf[l], page_table), 'gdn': state_gdn[l],
                'conv': state_gdn_conv[l], 'ret': state_ret[l]}
        xp, sp = decoder_layer(xp, c_prompt, pos_p, p, None)
        xs, ss = decoder_layer(xs, c_sample, pos_s, p, past)
        st_p.append(sp)
        st_s.append(ss)

    def stk(sts, i):
        return jnp.stack([s[i] for s in sts])

    return (xp, xs,
            stk(st_p, 0), stk(st_p, 1), stk(st_p, 2), stk(st_p, 3), stk(st_p, 4), stk(st_p, 5),
            stk(st_s, 0), stk(st_s, 1), stk(st_s, 2), stk(st_s, 3), stk(st_s, 4), stk(st_s, 5))
```

```python
import functools
import math

import jax
import jax.numpy as jnp
from jax import lax
from jax.experimental import pallas as pl
from jax.experimental.pallas import tpu as pltpu

F32 = jnp.float32
BF16 = jnp.bfloat16

HEAD_DIM = 128
N_HEADS = 8
MIX_W = N_HEADS * HEAD_DIM
N_BRANCH = 3
CONV_W = 4
CHUNK = 64
PAGE_SIZE = 128
ROPE_BASE = 10000.0
N_EXPERTS = 32
TOP_K = 4
SWIGLU_LIMIT = 7.0
SWIGLU_ALPHA = 1.702
EPS = 1e-6

LANES = 128
SUBLANES = 8
VMEM_LIMIT = 48 * 1024 * 1024
MOE_BLK = 256
NEG_BIG = -1e30


def _params(sem, vmem=VMEM_LIMIT):
    return pltpu.CompilerParams(dimension_semantics=sem, vmem_limit_bytes=vmem)


def _dot(a, b):
    return jnp.dot(a, b, preferred_element_type=F32)


def _dot_nt(a, b):
    return lax.dot_general(a, b, (((1,), (1,)), ((), ())), preferred_element_type=F32)


def _dot_tn(a, b):
    return lax.dot_general(a, b, (((0,), (0,)), ((), ())), preferred_element_type=F32)


def _split2(a):
    hi = a.astype(BF16)
    lo = (a - hi.astype(F32)).astype(BF16)
    return hi, lo


def _split3(a):
    hi = a.astype(BF16)
    r = a - hi.astype(F32)
    mid = r.astype(BF16)
    lo = (r - mid.astype(F32)).astype(BF16)
    return hi, mid, lo


def _dot_exact_lhs(mask_bf16, x):
    hi, mid, lo = _split3(x)
    return _dot(mask_bf16, hi) + (_dot(mask_bf16, mid) + _dot(mask_bf16, lo))


def _dot_exact_rhs(x, mask_bf16):
    hi, mid, lo = _split3(x)
    return _dot(hi, mask_bf16) + (_dot(mid, mask_bf16) + _dot(lo, mask_bf16))


def _sigmoid(x):
    return 1.0 / (1.0 + jnp.exp(-x))


def _silu(x):
    return x * _sigmoid(x)


def _softplus(x):
    return jnp.maximum(x, 0.0) + jnp.log1p(jnp.exp(-jnp.abs(x)))


def _iota2(shape, dim):
    return lax.broadcasted_iota(jnp.int32, shape, dim)


def _modulation_kernel(c_ref, w_ref, b_ref, o_ref):
    c = _silu(c_ref[...])
    c_hi, c_lo = _split2(c)
    w_hi, w_lo = _split2(w_ref[...])
    acc = _dot(c_hi, w_hi) + (_dot(c_hi, w_lo) + _dot(c_lo, w_hi))
    o_ref[...] = acc + b_ref[...]


def _modulation(c_all, w_ada, b_ada, tn=1024):
    depth, d, n = w_ada.shape
    rows = c_all.shape[0]
    return pl.pallas_call(
        _modulation_kernel,
        out_shape=jax.ShapeDtypeStruct((depth, rows, n), F32),
        grid=(depth, n // tn),
        in_specs=[
            pl.BlockSpec((rows, d), lambda l, j: (0, 0)),
            pl.BlockSpec((None, d, tn), lambda l, j: (l, 0, j)),
            pl.BlockSpec((None, 1, tn), lambda l, j: (l, 0, j)),
        ],
        out_specs=pl.BlockSpec((None, rows, tn), lambda l, j: (l, 0, j)),
        compiler_params=_params(("arbitrary", "arbitrary")),
        name="adaln_modulation",
    )(c_all, w_ada, b_ada.reshape(depth, 1, n))


def _rms(x):
    return x * lax.rsqrt(jnp.mean(x * x, axis=-1, keepdims=True) + EPS)


def _norm_mod_kernel(x_ref, w_ref, sc_ref, sh_ref, o_ref):
    y = _rms(x_ref[...]) * w_ref[...]
    o_ref[...] = (y * (1.0 + sc_ref[...]) + sh_ref[...]).astype(o_ref.dtype)


def _mod_spec(mod, tm, rows_per_group):
    r, d = mod.shape[1], mod.shape[2]
    tiles = rows_per_group // tm
    return pl.BlockSpec((None, r, d), lambda i: (i // tiles, 0, 0))


def _norm_mod(x, w, sc, sh, tm, rows_per_group, out_dtype=BF16):
    m, d = x.shape
    return pl.pallas_call(
        _norm_mod_kernel,
        out_shape=jax.ShapeDtypeStruct((m, d), out_dtype),
        grid=(m // tm,),
        in_specs=[
            pl.BlockSpec((tm, d), lambda i: (i, 0)),
            pl.BlockSpec((1, d), lambda i: (0, 0)),
            _mod_spec(sc, tm, rows_per_group),
            _mod_spec(sh, tm, rows_per_group),
        ],
        out_specs=pl.BlockSpec((tm, d), lambda i: (i, 0)),
        compiler_params=_params(("arbitrary",)),
        name="norm_modulate",
    )(x, w.reshape(1, d), sc, sh)


def _mm_kernel(a_ref, w_ref, o_ref, w16_ref):
    @pl.when(pl.program_id(1) == 0)
    def _():
        w16_ref[...] = w_ref[...].astype(BF16)

    o_ref[...] = _dot(a_ref[...], w16_ref[...]).astype(o_ref.dtype)


def _mm_residual_kernel(a_ref, w_ref, res_ref, gate_ref, o_ref, w16_ref):
    @pl.when(pl.program_id(1) == 0)
    def _():
        w16_ref[...] = w_ref[...].astype(BF16)

    o_ref[...] = res_ref[...] + gate_ref[...] * _dot(a_ref[...], w16_ref[...])


def _mm(a, w, tm, tn, out_dtype=F32):
    m, k = a.shape
    n = w.shape[1]
    return pl.pallas_call(
        _mm_kernel,
        out_shape=jax.ShapeDtypeStruct((m, n), out_dtype),
        grid=(n // tn, m // tm),
        in_specs=[
            pl.BlockSpec((tm, k), lambda j, i: (i, 0)),
            pl.BlockSpec((k, tn), lambda j, i: (0, j)),
        ],
        out_specs=pl.BlockSpec((tm, tn), lambda j, i: (i, j)),
        scratch_shapes=[pltpu.VMEM((k, tn), BF16)],
        compiler_params=_params(("arbitrary", "arbitrary")),
        name="matmul",
    )(a, w)


def _mm_residual(a, w, res, gate, tm, tn, rows_per_group):
    m, k = a.shape
    n = w.shape[1]
    r = gate.shape[1]
    tiles = rows_per_group // tm
    return pl.pallas_call(
        _mm_residual_kernel,
        out_shape=jax.ShapeDtypeStruct((m, n), F32),
        grid=(n // tn, m // tm),
        in_specs=[
            pl.BlockSpec((tm, k), lambda j, i: (i, 0)),
            pl.BlockSpec((k, tn), lambda j, i: (0, j)),
            pl.BlockSpec((tm, tn), lambda j, i: (i, j)),
            pl.BlockSpec((None, r, tn), lambda j, i: (i // tiles, 0, j)),
        ],
        out_specs=pl.BlockSpec((tm, tn), lambda j, i: (i, j)),
        scratch_shapes=[pltpu.VMEM((k, tn), BF16)],
        compiler_params=_params(("arbitrary", "arbitrary")),
        name="matmul_residual",
    )(a, w, res, gate)


def _gates_kernel(h_ref, whi_ref, wlo_ref, prm_ref, o_ref):
    h = h_ref[...]
    z = _dot(h, whi_ref[...]) + _dot(h, wlo_ref[...]) + prm_ref[0:1, :]
    lane = _iota2(z.shape, 1)
    tail = jnp.log1p(jnp.exp(-jnp.abs(z)))
    lf = jnp.minimum(z, 0.0) - tail
    g = prm_ref[1:2, :] * (jnp.maximum(z, 0.0) + tail)
    beta = _sigmoid(z)
    o_ref[...] = jnp.where(lane < 8, lf, jnp.where(lane < 16, g, beta))


def _gates(h, w_hi, w_lo, prm, tm):
    m, d = h.shape
    return pl.pallas_call(
        _gates_kernel,
        out_shape=jax.ShapeDtypeStruct((m, LANES), F32),
        grid=(m // tm,),
        in_specs=[
            pl.BlockSpec((tm, d), lambda i: (i, 0)),
            pl.BlockSpec((d, LANES), lambda i: (0, 0)),
            pl.BlockSpec((d, LANES), lambda i: (0, 0)),
            pl.BlockSpec((SUBLANES, LANES), lambda i: (0, 0)),
        ],
        out_specs=pl.BlockSpec((tm, LANES), lambda i: (i, 0)),
        compiler_params=_params(("arbitrary",)),
        name="gate_projection",
    )(h, w_hi, w_lo, prm)


def _cumsum_kernel(x_ref, o_ref, carry_ref, *, group):
    tm = x_ref.shape[0]
    r = _iota2((tm, tm), 0)
    c = _iota2((tm, tm), 1)
    if group >= tm:
        tri = jnp.where(c <= r, 1.0, 0.0).astype(BF16)
    else:
        tri = jnp.where(c <= r, jnp.where(r // group == c // group, 1.0, 0.0), 0.0).astype(BF16)
    cs = _dot_exact_lhs(tri, x_ref[...])
    if group > tm:
        @pl.when(pl.program_id(0) % (group // tm) == 0)
        def _():
            carry_ref[...] = jnp.zeros_like(carry_ref)

        cs = cs + carry_ref[...]
        carry_ref[...] = cs[tm - 1:tm, :]
    o_ref[...] = cs


def _cumsum_rows(x, tm, group):
    m, n = x.shape
    return pl.pallas_call(
        functools.partial(_cumsum_kernel, group=group),
        out_shape=jax.ShapeDtypeStruct((m, n), F32),
        grid=(m // tm,),
        in_specs=[pl.BlockSpec((tm, n), lambda i: (i, 0))],
        out_specs=pl.BlockSpec((tm, n), lambda i: (i, 0)),
        scratch_shapes=[pltpu.VMEM((1, n), F32)],
        compiler_params=_params(("arbitrary",)),
        name="cumsum_rows",
    )(x)


def _fox_prep_kernel(q_ref, k_ref, v_ref, qw_ref, kw_ref, fq16_ref, fk_ref, fk16_ref, fv16_ref):
    for h in range(N_HEADS):
        sl = slice(h * HEAD_DIM, (h + 1) * HEAD_DIM)
        fq16_ref[:, sl] = (_rms(q_ref[:, sl]) * qw_ref[...]).astype(BF16)
        kn = _rms(k_ref[:, sl]) * kw_ref[...]
        fk_ref[:, sl] = kn
        fk16_ref[:, sl] = kn.astype(BF16)
    fv16_ref[...] = v_ref[...].astype(BF16)


def _fox_prep(qkv, qw, kw, tm):
    m = qkv.shape[0]
    col = lambda c: pl.BlockSpec((tm, MIX_W), lambda i: (i, c))
    wspec = pl.BlockSpec((1, HEAD_DIM), lambda i: (0, 0))
    return pl.pallas_call(
        _fox_prep_kernel,
        out_shape=(
            jax.ShapeDtypeStruct((m, MIX_W), BF16),
            jax.ShapeDtypeStruct((m, MIX_W), F32),
            jax.ShapeDtypeStruct((m, MIX_W), BF16),
            jax.ShapeDtypeStruct((m, MIX_W), BF16),
        ),
        grid=(m // tm,),
        in_specs=[col(0), col(1), col(2), wspec, wspec],
        out_specs=(col(0), col(0), col(0), col(0)),
        compiler_params=_params(("arbitrary",)),
        name="fox_qk_norm",
    )(qkv, qkv, qkv, qw.reshape(1, HEAD_DIM), kw.reshape(1, HEAD_DIM))


def _online_softmax_update(s, v16, m_ref, l_ref, acc_ref):
    m_prev = m_ref[...]
    m_new = jnp.maximum(m_prev, jnp.max(s, axis=1, keepdims=True))
    alpha = jnp.exp(m_prev - m_new)
    p = jnp.exp(s - m_new)
    l_ref[...] = alpha * l_ref[...] + jnp.sum(p, axis=1, keepdims=True)
    acc_ref[...] = alpha * acc_ref[...] + _dot(p.astype(BF16), v16)
    m_ref[...] = m_new


def _flash_kernel(q_ref, k_ref, v_ref, cq_ref, ck_ref, o_ref, m_ref, l_ref, acc_ref, *, scale):
    i = pl.program_id(2)
    j = pl.program_id(3)

    @pl.when(j == 0)
    def _():
        m_ref[...] = jnp.full_like(m_ref, NEG_BIG)
        l_ref[...] = jnp.zeros_like(l_ref)
        acc_ref[...] = jnp.zeros_like(acc_ref)

    @pl.when(j <= i)
    def _():
        s = _dot_nt(q_ref[...], k_ref[...]) * scale + (cq_ref[...] - ck_ref[...])
        tq, tk = s.shape
        row = i * tq + _iota2(s.shape, 0)
        col = j * tk + _iota2(s.shape, 1)
        s = jnp.where(row >= col, s, NEG_BIG)
        _online_softmax_update(s, v_ref[...], m_ref, l_ref, acc_ref)

    @pl.when(j == i)
    def _():
        o_ref[...] = (acc_ref[...] / l_ref[...]).astype(o_ref.dtype)


def _flash_attention(fq16, fk16, fv16, cum_col, cum_row, batch, seq, t):
    n = seq // t
    qspec = pl.BlockSpec((t, HEAD_DIM), lambda b, h, i, j: (b * n + i, h))
    kspec = pl.BlockSpec((t, HEAD_DIM), lambda b, h, i, j: (b * n + jnp.minimum(j, i), h))
    return pl.pallas_call(
        functools.partial(_flash_kernel, scale=HEAD_DIM ** -0.5),
        out_shape=jax.ShapeDtypeStruct((batch * seq, MIX_W), BF16),
        grid=(batch, N_HEADS, n, n),
        in_specs=[
            qspec, kspec, kspec,
            pl.BlockSpec((None, None, t, 1), lambda b, h, i, j: (b, h, i, 0)),
            pl.BlockSpec((None, None, 1, t), lambda b, h, i, j: (b, h, 0, jnp.minimum(j, i))),
        ],
        out_specs=qspec,
        scratch_shapes=[pltpu.VMEM((t, 1), F32), pltpu.VMEM((t, 1), F32), pltpu.VMEM((t, HEAD_DIM), F32)],
        compiler_params=_params(("arbitrary", "arbitrary", "arbitrary", "arbitrary")),
        name="fox_flash_attention",
    )(fq16, fk16, fv16, cum_col, cum_row)


def _suffix_kernel(pt_ref, x_ref, o_ref, carry_ref):
    @pl.when(pl.program_id(1) == 0)
    def _():
        carry_ref[...] = jnp.zeros_like(carry_ref)

    x = x_ref[...]
    t = x.shape[1]
    later = jnp.where(_iota2((t, t), 0) > _iota2((t, t), 1), 1.0, 0.0).astype(BF16)
    suf = _dot_exact_rhs(x, later) + carry_ref[...]
    o_ref[...] = suf
    carry_ref[...] = suf[:, 0:1] + x[:, 0:1]


def _suffix_sums(page_table_flat, logf_t, layer, n_batch, n_pages):
    last = n_pages - 1
    return pl.pallas_call(
        _suffix_kernel,
        out_shape=jax.ShapeDtypeStruct((n_batch, n_pages, N_HEADS, PAGE_SIZE), F32),
        grid_spec=pltpu.PrefetchScalarGridSpec(
            num_scalar_prefetch=1,
            grid=(n_batch, n_pages),
            in_specs=[pl.BlockSpec((None, None, N_HEADS, PAGE_SIZE),
                                   lambda b, p, pt: (layer, pt[b * n_pages + last - p], 0, 0))],
            out_specs=pl.BlockSpec((None, None, N_HEADS, PAGE_SIZE), lambda b, p, pt: (b, last - p, 0, 0)),
            scratch_shapes=[pltpu.VMEM((N_HEADS, 1), F32)],
        ),
        compiler_params=_params(("arbitrary", "arbitrary")),
        name="fox_past_suffix",
    )(page_table_flat, logf_t)


def _decode_kernel(pt_ref, q_ref, k_ref, v_ref, bias_ref, cq_ref, kn_ref, vn_ref, ckn_ref, o_ref,
                   m_ref, l_ref, acc_ref, *, scale):
    p = pl.program_id(1)

    @pl.when(p == 0)
    def _():
        m_ref[...] = jnp.full_like(m_ref, NEG_BIG)
        l_ref[...] = jnp.zeros_like(l_ref)
        acc_ref[...] = jnp.zeros_like(acc_ref)

    q = q_ref[...]
    s = _dot_nt(q, k_ref[...].astype(BF16)) * scale + (cq_ref[...] + bias_ref[...])
    same_head = _iota2(s.shape, 1) % N_HEADS == _iota2(s.shape, 0) // SUBLANES
    s = jnp.where(same_head, s, NEG_BIG)
    _online_softmax_update(s, v_ref[...].astype(BF16), m_ref, l_ref, acc_ref)

    @pl.when(p == pl.num_programs(1) - 1)
    def _():
        s2 = _dot_nt(q, kn_ref[...]) * scale + (cq_ref[...] - ckn_ref[...])
        row = _iota2(s2.shape, 0)
        lane = _iota2(s2.shape, 1)
        s2 = jnp.where(lane // N_HEADS <= row % SUBLANES, s2, NEG_BIG)
        s2 = jnp.where(lane % N_HEADS == row // SUBLANES, s2, NEG_BIG)
        _online_softmax_update(s2, vn_ref[...], m_ref, l_ref, acc_ref)
        o_ref[...] = acc_ref[...] / l_ref[...]


def _decode_attention(page_table_flat, q64, cache_k, cache_v, bias, cq, kn, vn, ckn, layer, n_batch, n_pages):
    rows = N_HEADS * SUBLANES
    flat = PAGE_SIZE * N_HEADS
    page = pl.BlockSpec((None, None, flat, HEAD_DIM), lambda b, p, pt: (layer, pt[b * n_pages + p], 0, 0))
    per_b = lambda shape: pl.BlockSpec((None,) + shape, lambda b, p, pt: (b, 0, 0))
    return pl.pallas_call(
        functools.partial(_decode_kernel, scale=HEAD_DIM ** -0.5),
        out_shape=jax.ShapeDtypeStruct((n_batch, rows, HEAD_DIM), F32),
        grid_spec=pltpu.PrefetchScalarGridSpec(
            num_scalar_prefetch=1,
            grid=(n_batch, n_pages),
            in_specs=[
                per_b((rows, HEAD_DIM)), page, page,
                pl.BlockSpec((None, None, 1, flat), lambda b, p, pt: (b, p, 0, 0)),
                per_b((rows, 1)),
                per_b((LANES, HEAD_DIM)), per_b((LANES, HEAD_DIM)), per_b((1, LANES)),
            ],
            out_specs=per_b((rows, HEAD_DIM)),
            scratch_shapes=[pltpu.VMEM((rows, 1), F32), pltpu.VMEM((rows, 1), F32),
                            pltpu.VMEM((rows, HEAD_DIM), F32)],
        ),
        compiler_params=_params(("arbitrary", "arbitrary")),
        name="fox_paged_decode",
    )(page_table_flat, q64, cache_k, cache_v, bias, cq, kn, vn, ckn)


def _unit_lower_inverse(m, c):
    r = _iota2((c, c), 0)
    col = _iota2((c, c), 1)
    inv = jnp.where(r == col, 1.0, 0.0) - jnp.where(r // 2 == col // 2, m, 0.0)
    s = 2
    while s < c:
        off = jnp.where(r // (2 * s) == col // (2 * s), jnp.where(r // s != col // s, m, 0.0), 0.0)
        inv16 = inv.astype(BF16)
        inv = inv - _dot(inv16, _dot(off.astype(BF16), inv16).astype(BF16))
        s *= 2
    return inv


def _gdn_kernel(qkv_ref, convw_ref, conv0_ref, gates_ref, gates_t_ref, z_ref, nw_ref, s0_ref,
                o_ref, st_ref, ext_ref, s_ref, *, c):
    n = pl.program_id(1)

    @pl.when(n == 0)
    def _():
        ext_ref[0:SUBLANES, :] = conv0_ref[...]
        s_ref[...] = s0_ref[...]

    ext_ref[SUBLANES:SUBLANES + c, :] = qkv_ref[...]

    def conv(c0):
        acc = ext_ref[SUBLANES:SUBLANES + c, c0:c0 + HEAD_DIM] * convw_ref[CONV_W - 1:CONV_W, c0:c0 + HEAD_DIM]
        for i in range(1, CONV_W):
            acc = acc + (ext_ref[SUBLANES - i:SUBLANES - i + c, c0:c0 + HEAD_DIM]
                         * convw_ref[CONV_W - 1 - i:CONV_W - i, c0:c0 + HEAD_DIM])
        return _silu(acc)

    def l2n(x):
        return x * lax.rsqrt(jnp.sum(x * x, axis=-1, keepdims=True) + EPS)

    r = _iota2((c, c), 0)
    col = _iota2((c, c), 1)
    gates = gates_ref[...]
    incl_lower = jnp.where(col <= r, 1.0, 0.0).astype(BF16)
    incl_upper = jnp.where(r <= col, 1.0, 0.0).astype(BF16)
    gc_cols = _dot_exact_lhs(incl_lower, gates)
    gc_rows = _dot_exact_rhs(gates_t_ref[...], incl_upper)

    for h in range(N_HEADS):
        q = l2n(conv(h * HEAD_DIM)) * (HEAD_DIM ** -0.5)
        k = l2n(conv(MIX_W + h * HEAD_DIM))
        v = conv(2 * MIX_W + h * HEAD_DIM)
        beta = gates[:, 16 + h:17 + h]
        gc = gc_cols[:, 8 + h:9 + h]
        gc_row = gc_rows[8 + h:9 + h, :]
        gl = gc[c - 1:c, :]
        decay = jnp.exp(jnp.where(col <= r, gc - gc_row, NEG_BIG))
        kb = k * beta
        k16 = k.astype(BF16)
        m = jnp.where(col < r, _dot_nt(kb.astype(BF16), k16) * decay, 0.0)
        t16 = _unit_lower_inverse(m, c).astype(BF16)
        u0 = _dot(t16, (v * beta).astype(BF16))
        wk = _dot(t16, (kb * jnp.exp(gc)).astype(BF16))
        a_qk = _dot_nt(q.astype(BF16), k16) * decay
        s = s_ref[h]
        s16 = s.astype(BF16)
        u = u0 - _dot(wk.astype(BF16), s16)
        u16 = u.astype(BF16)
        o = _dot((q * jnp.exp(gc)).astype(BF16), s16) + _dot(a_qk.astype(BF16), u16)
        s_ref[h] = s * jnp.exp(gl) + _dot_tn((k * jnp.exp(gl - gc)).astype(BF16), u16)
        zh = z_ref[:, h * HEAD_DIM:(h + 1) * HEAD_DIM]
        o_ref[:, h * HEAD_DIM:(h + 1) * HEAD_DIM] = (_rms(o) * nw_ref[...] * _silu(zh)).astype(o_ref.dtype)

    ext_ref[0:SUBLANES, :] = ext_ref[c:c + SUBLANES, :]

    @pl.when(n == pl.num_programs(1) - 1)
    def _():
        st_ref[...] = s_ref[...]


def _gdn(qkv, conv_w, conv0, gates, gates_t, z, norm_w, s0, batch, n_chunks, c, out_dtype):
    m = qkv.shape[0]
    w3 = 3 * MIX_W
    row = lambda width: pl.BlockSpec((c, width), lambda b, n: (b * n_chunks + n, 0))
    state = pl.BlockSpec((None, N_HEADS, HEAD_DIM, HEAD_DIM), lambda b, n: (b, 0, 0, 0))
    return pl.pallas_call(
        functools.partial(_gdn_kernel, c=c),
        out_shape=(jax.ShapeDtypeStruct((m, MIX_W), out_dtype),
                   jax.ShapeDtypeStruct((batch, N_HEADS, HEAD_DIM, HEAD_DIM), F32)),
        grid=(batch, n_chunks),
        in_specs=[
            row(w3),
            pl.BlockSpec((CONV_W, w3), lambda b, n: (0, 0)),
            pl.BlockSpec((None, SUBLANES, w3), lambda b, n: (b, 0, 0)),
            row(LANES),
            pl.BlockSpec((None, LANES, c), lambda b, n: (b * n_chunks + n, 0, 0)),
            row(MIX_W),
            pl.BlockSpec((1, HEAD_DIM), lambda b, n: (0, 0)),
            state,
        ],
        out_specs=(row(MIX_W), state),
        scratch_shapes=[pltpu.VMEM((c + SUBLANES, w3), F32),
                        pltpu.VMEM((N_HEADS, HEAD_DIM, HEAD_DIM), F32)],
        compiler_params=_params(("arbitrary", "arbitrary")),
        name="gated_deltanet",
    )(qkv, conv_w, conv0, gates, gates_t, z, norm_w.reshape(1, HEAD_DIM), s0)


def _ret_kernel(q_ref, k_ref, v_ref, g_ref, cos_ref, sin_ref, s0_ref, o_ref, st_ref, s_ref, *, c, n_valid):
    n = pl.program_id(1)

    @pl.when(n == 0)
    def _():
        s_ref[...] = s0_ref[...]

    cos = cos_ref[...]
    sin = sin_ref[...]

    def rot(x):
        return x * cos + pltpu.roll(x, HEAD_DIM // 2, 1) * sin

    r = _iota2((c, c), 0)
    col = _iota2((c, c), 1)
    cnt_r = jnp.minimum(r + 1, n_valid).astype(F32)
    cnt_c = jnp.minimum(col + 1, n_valid).astype(F32)
    cnt = cnt_r[:, 0:1]
    total = float(min(c, n_valid))

    for h in range(N_HEADS):
        lg = math.log(1.0 - 2.0 ** (-5.0 - h))
        sl = slice(h * HEAD_DIM, (h + 1) * HEAD_DIM)
        q = rot(q_ref[:, sl])
        k = rot(k_ref[:, sl]) * (HEAD_DIM ** -0.5)
        v16 = v_ref[:, sl].astype(BF16)
        decay = jnp.exp(jnp.where(col <= r, (cnt_r - cnt_c) * lg, NEG_BIG))
        a = _dot_nt(q.astype(BF16), k.astype(BF16)) * decay
        s = s_ref[h]
        o = _dot(a.astype(BF16), v16) + _dot((q * jnp.exp(cnt * lg)).astype(BF16), s.astype(BF16))
        s_ref[h] = s * math.exp(total * lg) + _dot_tn((k * jnp.exp((total - cnt) * lg)).astype(BF16), v16)
        o_ref[:, sl] = (_rms(o) * _silu(g_ref[:, sl])).astype(o_ref.dtype)

    @pl.when(n == pl.num_programs(1) - 1)
    def _():
        st_ref[...] = s_ref[...]


def _retention(qkvg, cos, sin, s0, batch, n_chunks, c, n_valid, out_dtype):
    m = qkvg.shape[0]
    colblk = lambda j: pl.BlockSpec((c, MIX_W), lambda b, n: (b * n_chunks + n, j))
    table = pl.BlockSpec((c, HEAD_DIM), lambda b, n: (n, 0))
    state = pl.BlockSpec((None, N_HEADS, HEAD_DIM, HEAD_DIM), lambda b, n: (b, 0, 0, 0))
    return pl.pallas_call(
        functools.partial(_ret_kernel, c=c, n_valid=n_valid),
        out_shape=(jax.ShapeDtypeStruct((m, MIX_W), out_dtype),
                   jax.ShapeDtypeStruct((batch, N_HEADS, HEAD_DIM, HEAD_DIM), F32)),
        grid=(batch, n_chunks),
        in_specs=[colblk(0), colblk(1), colblk(2), colblk(3), table, table, state],
        out_specs=(colblk(0), state),
        scratch_shapes=[pltpu.VMEM((N_HEADS, HEAD_DIM, HEAD_DIM), F32)],
        compiler_params=_params(("arbitrary", "arbitrary")),
        name="retention",
    )(qkvg, qkvg, qkvg, qkvg, cos, sin, s0)


def _merge_kernel(fox_ref, gdn_ref, ret_ref, w_ref, g0_ref, g1_ref, g2_ref, o_ref, w16_ref):
    @pl.when(pl.program_id(1) == 0)
    def _():
        w16_ref[...] = w_ref[...].astype(BF16)

    acc = _sigmoid(g0_ref[...]) * _dot(fox_ref[...], w16_ref[0])
    acc = acc + _sigmoid(g1_ref[...]) * _dot(gdn_ref[...], w16_ref[1])
    acc = acc + _sigmoid(g2_ref[...]) * _dot(ret_ref[...], w16_ref[2])
    o_ref[...] = acc.astype(o_ref.dtype)


def _merge(o_fox, o_gdn, o_ret, w_branch, gate, tm, tn):
    m = o_fox.shape[0]
    d = w_branch.shape[2]
    nj = d // tn
    br = pl.BlockSpec((tm, MIX_W), lambda j, i: (i, 0))
    gspec = lambda b: pl.BlockSpec((tm, tn), lambda j, i: (i, b * nj + j))
    return pl.pallas_call(
        _merge_kernel,
        out_shape=jax.ShapeDtypeStruct((m, d), BF16),
        grid=(nj, m // tm),
        in_specs=[br, br, br,
                  pl.BlockSpec((N_BRANCH, MIX_W, tn), lambda j, i: (0, 0, j)),
                  gspec(0), gspec(1), gspec(2)],
        out_specs=pl.BlockSpec((tm, tn), lambda j, i: (i, j)),
        scratch_shapes=[pltpu.VMEM((N_BRANCH, MIX_W, tn), BF16)],
        compiler_params=_params(("arbitrary", "arbitrary")),
        name="branch_merge",
    )(o_fox, o_gdn, o_ret, w_branch, gate, gate, gate)


def _router_kernel(h_ref, whi_ref, wlo_ref, b_ref, idx_ref, wt_ref):
    h_hi, h_lo = _split2(h_ref[...])
    logits = _dot(h_hi, whi_ref[...]) + (_dot(h_hi, wlo_ref[...]) + _dot(h_lo, whi_ref[...])) + b_ref[...]
    lane = _iota2(logits.shape, 1)
    lane_f = lane.astype(F32)
    logits = jnp.where(lane < N_EXPERTS, logits, -jnp.inf)
    idx = jnp.zeros(logits.shape, F32)
    vals = []
    for k in range(TOP_K):
        top = jnp.max(logits, axis=1, keepdims=True)
        sel = jnp.min(jnp.where(logits == top, lane_f, float(LANES)), axis=1, keepdims=True)
        idx = jnp.where(lane == k, sel, idx)
        vals.append(top)
        logits = jnp.where(lane_f == sel, -jnp.inf, logits)
    es = [jnp.exp(v - vals[0]) for v in vals]
    tot = es[0] + es[1] + es[2] + es[3]
    wt = jnp.zeros(logits.shape, F32)
    for k in range(TOP_K):
        wt = jnp.where(lane == k, es[k] / tot, wt)
    idx_ref[...] = idx.astype(jnp.int32)
    wt_ref[...] = wt


def _router(h, w_hi, w_lo, b, tm):
    m, d = h.shape
    return pl.pallas_call(
        _router_kernel,
        out_shape=(jax.ShapeDtypeStruct((m, LANES), jnp.int32), jax.ShapeDtypeStruct((m, LANES), F32)),
        grid=(m // tm,),
        in_specs=[
            pl.BlockSpec((tm, d), lambda i: (i, 0)),
            pl.BlockSpec((d, LANES), lambda i: (0, 0)),
            pl.BlockSpec((d, LANES), lambda i: (0, 0)),
            pl.BlockSpec((1, LANES), lambda i: (0, 0)),
        ],
        out_specs=(pl.BlockSpec((tm, LANES), lambda i: (i, 0)), pl.BlockSpec((tm, LANES), lambda i: (i, 0))),
        compiler_params=_params(("arbitrary",)),
        name="moe_router_topk",
    )(h, w_hi, w_lo, b)


def _row_copy(src_hbm, src_row, dst_vmem, dst_row, sem):
    return pltpu.make_async_copy(src_hbm.at[pl.ds(src_row, 1), :], dst_vmem.at[pl.ds(dst_row, 1), :], sem)


def _gather_kernel(tok_ref, nv_ref, h_ref, o_ref, buf_ref, sem):
    b = pl.program_id(0)
    blk = buf_ref.shape[0]

    @pl.when(b < nv_ref[0])
    def _():
        def issue(r, carry):
            _row_copy(h_ref, tok_ref[b * blk + r], buf_ref, r, sem).start()
            return carry

        lax.fori_loop(0, blk, issue, 0)

        def wait(r, carry):
            _row_copy(h_ref, 0, buf_ref, r, sem).wait()
            return carry

        lax.fori_loop(0, blk, wait, 0)
        o_ref[...] = buf_ref[...].astype(o_ref.dtype)

    @pl.when(b >= nv_ref[0])
    def _():
        o_ref[...] = jnp.zeros_like(o_ref)


def _gather_rows(slot_tok, n_valid, h_ext, n_blk, blk):
    d = h_ext.shape[1]
    return pl.pallas_call(
        _gather_kernel,
        out_shape=jax.ShapeDtypeStruct((n_blk * blk, d), BF16),
        grid_spec=pltpu.PrefetchScalarGridSpec(
            num_scalar_prefetch=2,
            grid=(n_blk,),
            in_specs=[pl.BlockSpec(memory_space=pl.ANY)],
            out_specs=pl.BlockSpec((blk, d), lambda b, tok, nv: (b, 0)),
            scratch_shapes=[pltpu.VMEM((blk, d), F32), pltpu.SemaphoreType.DMA],
        ),
        compiler_params=_params(("arbitrary",)),
        name="moe_token_gather",
    )(slot_tok, n_valid, h_ext)


def _expert_changed(be_ref, b):
    return jnp.logical_or(b == 0, be_ref[b] != be_ref[jnp.maximum(b - 1, 0)])


def _gmm1_kernel(be_ref, nv_ref, x_ref, wg_ref, wl_ref, bg_ref, bl_ref, o_ref, wg16_ref, wl16_ref):
    b = pl.program_id(1)

    @pl.when(_expert_changed(be_ref, b))
    def _():
        wg16_ref[...] = wg_ref[...].astype(BF16)
        wl16_ref[...] = wl_ref[...].astype(BF16)

    @pl.when(b < nv_ref[0])
    def _():
        x = x_ref[...]
        glu = jnp.minimum(_dot(x, wg16_ref[...]) + bg_ref[...], SWIGLU_LIMIT)
        lin = jnp.clip(_dot(x, wl16_ref[...]) + bl_ref[...], -SWIGLU_LIMIT, SWIGLU_LIMIT)
        o_ref[...] = (glu * _sigmoid(SWIGLU_ALPHA * glu) * (lin + 1.0)).astype(o_ref.dtype)

    @pl.when(b >= nv_ref[0])
    def _():
        o_ref[...] = jnp.zeros_like(o_ref)


def _gmm1(blk_e, n_valid, xs, w_e1, b_e1, n_blk, blk, tn):
    n_exp, d, two_f = w_e1.shape
    d_ff = two_f // 2
    nj = d_ff // tn
    rowblk = lambda width: pl.BlockSpec((blk, width), lambda j, b, be, nv: (jnp.minimum(b, nv[0] - 1), 0))
    return pl.pallas_call(
        _gmm1_kernel,
        out_shape=jax.ShapeDtypeStruct((n_blk * blk, d_ff), BF16),
        grid_spec=pltpu.PrefetchScalarGridSpec(
            num_scalar_prefetch=2,
            grid=(nj, n_blk),
            in_specs=[
                rowblk(d),
                pl.BlockSpec((None, d, tn), lambda j, b, be, nv: (be[b], 0, j)),
                pl.BlockSpec((None, d, tn), lambda j, b, be, nv: (be[b], 0, nj + j)),
                pl.BlockSpec((None, 1, tn), lambda j, b, be, nv: (be[b], 0, j)),
                pl.BlockSpec((None, 1, tn), lambda j, b, be, nv: (be[b], 0, nj + j)),
            ],
            out_specs=pl.BlockSpec((blk, tn), lambda j, b, be, nv: (b, j)),
            scratch_shapes=[pltpu.VMEM((d, tn), BF16), pltpu.VMEM((d, tn), BF16)],
        ),
        compiler_params=_params(("arbitrary", "arbitrary")),
        name="moe_expert_up",
    )(blk_e, n_valid, xs, w_e1, w_e1, b_e1.reshape(n_exp, 1, two_f), b_e1.reshape(n_exp, 1, two_f))


def _gmm2_kernel(be_ref, nv_ref, a_ref, w_ref, b_ref, o_ref, w16_ref):
    b = pl.program_id(1)

    @pl.when(_expert_changed(be_ref, b))
    def _():
        w16_ref[...] = w_ref[...].astype(BF16)

    @pl.when(b < nv_ref[0])
    def _():
        o_ref[...] = _dot(a_ref[...], w16_ref[...]) + b_ref[...]

    @pl.when(b >= nv_ref[0])
    def _():
        o_ref[...] = jnp.zeros_like(o_ref)


def _gmm2(blk_e, n_valid, act, w_e2, b_e2, n_blk, blk, tn):
    n_exp, d_ff, d = w_e2.shape
    return pl.pallas_call(
        _gmm2_kernel,
        out_shape=jax.ShapeDtypeStruct((n_blk * blk, d), F32),
        grid_spec=pltpu.PrefetchScalarGridSpec(
            num_scalar_prefetch=2,
            grid=(d // tn, n_blk),
            in_specs=[
                pl.BlockSpec((blk, d_ff), lambda j, b, be, nv: (jnp.minimum(b, nv[0] - 1), 0)),
                pl.BlockSpec((None, d_ff, tn), lambda j, b, be, nv: (be[b], 0, j)),
                pl.BlockSpec((None, 1, tn), lambda j, b, be, nv: (be[b], 0, j)),
            ],
            out_specs=pl.BlockSpec((blk, tn), lambda j, b, be, nv: (b, j)),
            scratch_shapes=[pltpu.VMEM((d_ff, tn), BF16)],
        ),
        compiler_params=_params(("arbitrary", "arbitrary")),
        name="moe_expert_down",
    )(blk_e, n_valid, act, w_e2, b_e2.reshape(n_exp, 1, d))


def _combine_kernel(pos_ref, ys_ref, wt_ref, x_ref, gate_ref, o_ref, buf_ref, sem):
    tm = x_ref.shape[0]
    base = pl.program_id(0) * tm * TOP_K

    def issue(t, carry):
        for k in range(TOP_K):
            _row_copy(ys_ref, pos_ref[base + t * TOP_K + k], buf_ref.at[k], t, sem).start()
        return carry

    lax.fori_loop(0, tm, issue, 0)

    def wait(t, carry):
        for k in range(TOP_K):
            _row_copy(ys_ref, 0, buf_ref.at[k], t, sem).wait()
        return carry

    lax.fori_loop(0, tm, wait, 0)
    wt = wt_ref[...]
    acc = buf_ref[0] * wt[:, 0:1]
    for k in range(1, TOP_K):
        acc = acc + buf_ref[k] * wt[:, k:k + 1]
    o_ref[...] = x_ref[...] + gate_ref[...] * acc


def _combine(pos, ys, wt, x, gate, tm, rows_per_group):
    m, d = x.shape
    r = gate.shape[1]
    tiles = rows_per_group // tm
    return pl.pallas_call(
        _combine_kernel,
        out_shape=jax.ShapeDtypeStruct((m, d), F32),
        grid_spec=pltpu.PrefetchScalarGridSpec(
            num_scalar_prefetch=1,
            grid=(m // tm,),
            in_specs=[
                pl.BlockSpec(memory_space=pl.ANY),
                pl.BlockSpec((tm, LANES), lambda i, pos: (i, 0)),
                pl.BlockSpec((tm, d), lambda i, pos: (i, 0)),
                pl.BlockSpec((None, r, d), lambda i, pos: (i // tiles, 0, 0)),
            ],
            out_specs=pl.BlockSpec((tm, d), lambda i, pos: (i, 0)),
            scratch_shapes=[pltpu.VMEM((TOP_K, tm, d), F32), pltpu.SemaphoreType.DMA],
        ),
        compiler_params=_params(("arbitrary",)),
        name="moe_combine",
    )(pos, ys, wt, x, gate)


TM = 512
TN = 1024


def _pad_lanes(a):
    return jnp.pad(a, ((0, 0), (0, LANES - a.shape[1])))


def _layer_weights(l, d, w_in, fox_f_bias, gdn_a_log, gdn_dt_bias, w_router, b_router):
    w = w_in[l]
    o_f = 3 * MIX_W
    o_gqkv = o_f + N_HEADS
    o_ga = o_gqkv + 3 * MIX_W
    o_gb = o_ga + N_HEADS
    o_gz = o_gb + N_HEADS
    o_ret = o_gz + MIX_W
    o_gate = o_ret + 4 * MIX_W
    small = _pad_lanes(jnp.concatenate([w[:, o_f:o_gqkv], w[:, o_ga:o_gb], w[:, o_gb:o_gz]], axis=1))
    ws_hi, ws_lo = _split2(small)
    prm = jnp.zeros((SUBLANES, LANES), F32)
    prm = prm.at[0, 0:8].set(fox_f_bias[l]).at[0, 8:16].set(gdn_dt_bias[l])
    prm = prm.at[1, 8:16].set(-jnp.exp(gdn_a_log[l]))
    wr_hi, wr_lo = _split2(_pad_lanes(w_router[l]))
    return dict(
        w_fox=w[:, 0:o_f].astype(BF16),
        w_gqkv=w[:, o_gqkv:o_ga].astype(BF16),
        w_gz=w[:, o_gz:o_ret].astype(BF16),
        w_ret=w[:, o_ret:o_gate].astype(BF16),
        w_gate=w[:, o_gate:o_gate + N_BRANCH * d].astype(BF16),
        ws_hi=ws_hi, ws_lo=ws_lo, prm=prm, wr_hi=wr_hi, wr_lo=wr_lo,
        b_router=_pad_lanes(b_router[l].reshape(1, N_EXPERTS)),
    )


def _rope_tables(pos):
    half = HEAD_DIM // 2
    inv = ROPE_BASE ** (-jnp.arange(half, dtype=F32) / half)
    ang = pos.astype(F32)[:, None] * inv[None, :]
    cos, sin = jnp.cos(ang), jnp.sin(ang)
    return jnp.concatenate([cos, cos], axis=1), jnp.concatenate([-sin, sin], axis=1)


def _project(h, lw, tm):
    fox = _mm(h, lw['w_fox'], tm, TN)
    gqkv = _mm(h, lw['w_gqkv'], tm, TN)
    gz = _mm(h, lw['w_gz'], tm, TN)
    ret = _mm(h, lw['w_ret'], tm, TN)
    gate = _mm(h, lw['w_gate'], tm, TN)
    gates = _gates(h, lw['ws_hi'], lw['ws_lo'], lw['prm'], tm)
    return fox, gqkv, gz, ret, gate, gates


def _prompt_mixer(x, mod, lw, p, batch, seq):
    m, d = x.shape
    sh1, sc1, g1, sh2, sc2, g2 = mod
    h = _norm_mod(x, p['norm1'], sc1, sh1, TM, seq)
    fox, gqkv, gz, ret, gate, gates = _project(h, lw, TM)
    cum = _cumsum_rows(gates, 256, seq)
    fq16, fk, fk16, fv16 = _fox_prep(fox, p['fox_qn'], p['fox_kn'], TM)
    cum_t = cum[:, :N_HEADS].reshape(batch, seq, N_HEADS).transpose(0, 2, 1)
    o_fox = _flash_attention(fq16, fk16, fv16, cum_t[..., None], cum_t[:, :, None, :], batch, seq, 512)
    n_chunks = seq // CHUNK
    gates_t = gates.reshape(m // CHUNK, CHUNK, LANES).transpose(0, 2, 1)
    zero_state = jnp.zeros((batch, N_HEADS, HEAD_DIM, HEAD_DIM), F32)
    o_gdn, s_gdn = _gdn(gqkv, p['gdn_conv'], jnp.zeros((batch, SUBLANES, 3 * MIX_W), F32), gates, gates_t, gz,
                        p['gdn_nw'], zero_state, batch, n_chunks, CHUNK, BF16)
    cos, sin = _rope_tables(jnp.arange(seq, dtype=jnp.int32))
    o_ret, s_ret = _retention(ret, cos, sin, zero_state, batch, n_chunks, CHUNK, CHUNK, BF16)
    merged = _merge(o_fox, o_gdn, o_ret, p['w_branch'], gate, TM, 512)
    x1 = _mm_residual(merged, p['w_out'], x, g1, TM, TN, seq)
    h2 = _norm_mod(x1, p['norm2'], sc2, sh2, TM, seq, F32)
    state = (fk.reshape(batch, seq, N_HEADS, HEAD_DIM),
             fox[:, 2 * MIX_W:].reshape(batch, seq, N_HEADS, HEAD_DIM),
             gates[:, :N_HEADS].reshape(batch, seq, N_HEADS),
             s_gdn,
             gqkv.reshape(batch, seq, 3 * MIX_W)[:, seq - (CONV_W - 1):],
             s_ret)
    return x1, h2, state


def _sample_mixer(x, mod, lw, p, past, layer, batch, seq, past_len):
    m, d = x.shape
    sh1, sc1, g1, sh2, sc2, g2 = mod
    n_pages = past_len // PAGE_SIZE
    h = _norm_mod(x, p['norm1'], sc1, sh1, m, m)
    fox, gqkv, gz, ret, gate, gates = _project(h, lw, m)
    cum = _cumsum_rows(gates, m, seq)
    fq16, fk, fk16, fv16 = _fox_prep(fox, p['fox_qn'], p['fox_kn'], m)

    pad_q = SUBLANES - seq
    q64 = jnp.pad(fq16.reshape(batch, seq, N_HEADS, HEAD_DIM).transpose(0, 2, 1, 3),
                  ((0, 0), (0, 0), (0, pad_q), (0, 0))).reshape(batch, N_HEADS * SUBLANES, HEAD_DIM)
    cum8 = cum[:, :N_HEADS].reshape(batch, seq, N_HEADS)
    cq = jnp.pad(cum8.transpose(0, 2, 1), ((0, 0), (0, 0), (0, pad_q))).reshape(batch, N_HEADS * SUBLANES, 1)
    new_rows = seq * N_HEADS
    kn = jnp.pad(fk16.reshape(batch, new_rows, HEAD_DIM), ((0, 0), (0, LANES - new_rows), (0, 0)))
    vn = jnp.pad(fv16.reshape(batch, new_rows, HEAD_DIM), ((0, 0), (0, LANES - new_rows), (0, 0)))
    ckn = jnp.pad(cum8.reshape(batch, 1, new_rows), ((0, 0), (0, 0), (0, LANES - new_rows)))
    suf = _suffix_sums(past['pt'], past['logf_t'], layer, batch, n_pages)
    bias = suf.transpose(0, 1, 3, 2).reshape(batch, n_pages, 1, PAGE_SIZE * N_HEADS)
    o = _decode_attention(past['pt'], q64, past['k'], past['v'], bias, cq, kn, vn, ckn, layer, batch, n_pages)
    o_fox = (o.reshape(batch, N_HEADS, SUBLANES, HEAD_DIM)[:, :, :seq].transpose(0, 2, 1, 3)
             .reshape(m, MIX_W).astype(BF16))

    def pad_rows(a):
        return jnp.pad(a.reshape(batch, seq, -1), ((0, 0), (0, pad_q), (0, 0))).reshape(batch * SUBLANES, -1)

    def unpad_rows(a):
        return a.reshape(batch, SUBLANES, -1)[:, :seq].reshape(m, -1)

    gates_p = pad_rows(gates)
    gates_t = gates_p.reshape(batch, SUBLANES, LANES).transpose(0, 2, 1)
    conv0 = jnp.pad(past['conv'], ((0, 0), (SUBLANES - (CONV_W - 1), 0), (0, 0)))
    o_gdn, s_gdn = _gdn(pad_rows(gqkv), p['gdn_conv'], conv0, gates_p, gates_t, pad_rows(gz), p['gdn_nw'],
                        past['gdn'], batch, 1, SUBLANES, F32)
    cos, sin = _rope_tables(past_len + jnp.arange(seq, dtype=jnp.int32))
    cos = jnp.pad(cos, ((0, pad_q), (0, 0)))
    sin = jnp.pad(sin, ((0, pad_q), (0, 0)))
    o_ret, s_ret = _retention(pad_rows(ret), cos, sin, past['ret'], batch, 1, SUBLANES, seq, F32)
    o_gdn = unpad_rows(o_gdn).astype(BF16)
    o_ret = unpad_rows(o_ret).astype(BF16)

    merged = _merge(o_fox, o_gdn, o_ret, p['w_branch'], gate, m, 512)
    x1 = _mm_residual(merged, p['w_out'], x, g1, m, TN, m)
    h2 = _norm_mod(x1, p['norm2'], sc2, sh2, m, m, F32)
    state = (fk.reshape(batch, seq, N_HEADS, HEAD_DIM),
             fox[:, 2 * MIX_W:].reshape(batch, seq, N_HEADS, HEAD_DIM),
             gates[:, :N_HEADS].reshape(batch, seq, N_HEADS),
             s_gdn,
             gqkv.reshape(batch, seq, 3 * MIX_W)[:, seq - (CONV_W - 1):],
             s_ret)
    return x1, h2, state


def _moe(h2_all, lw, w_e1, b_e1, w_e2, b_e2):
    n, d = h2_all.shape
    idx, wt = _router(h2_all, lw['wr_hi'], lw['wr_lo'], lw['b_router'], LANES)
    flat_e = idx[:, :TOP_K].reshape(-1)
    n_assign = n * TOP_K
    onehot = (flat_e[:, None] == jnp.arange(N_EXPERTS, dtype=jnp.int32)[None, :]).astype(jnp.int32)
    csum = jnp.cumsum(onehot, axis=0)
    rank = jnp.sum((csum - onehot) * onehot, axis=1)
    counts = csum[-1]
    padded = (counts + MOE_BLK - 1) // MOE_BLK * MOE_BLK
    pad_end = jnp.cumsum(padded)
    dest = ((pad_end - padded)[flat_e] + rank).astype(jnp.int32)
    n_blk = -(-n_assign // MOE_BLK) + N_EXPERTS
    slot_tok = jnp.full((n_blk * MOE_BLK,), n, jnp.int32).at[dest].set(
        jnp.arange(n_assign, dtype=jnp.int32) // TOP_K)
    blk_ids = jnp.arange(n_blk, dtype=jnp.int32)
    blk_e = jnp.minimum(jnp.searchsorted(pad_end, blk_ids * MOE_BLK, side='right'), N_EXPERTS - 1)
    n_valid = (pad_end[-1] // MOE_BLK).astype(jnp.int32)
    blk_e = jnp.where(blk_ids < n_valid, blk_e, blk_e[n_valid - 1]).astype(jnp.int32)
    n_valid = n_valid.reshape(1)
    h_ext = jnp.concatenate([h2_all, jnp.zeros((SUBLANES, d), F32)], axis=0)
    xs = _gather_rows(slot_tok, n_valid, h_ext, n_blk, MOE_BLK)
    tn = min(512, w_e2.shape[1], w_e2.shape[2])
    act = _gmm1(blk_e, n_valid, xs, w_e1, b_e1, n_blk, MOE_BLK, tn)
    ys = _gmm2(blk_e, n_valid, act, w_e2, b_e2, n_blk, MOE_BLK, tn)
    return ys, dest, wt


def kernel(x_prompt, x_sample, cache_fox_k, cache_fox_v, cache_fox_logf, state_gdn, state_gdn_conv, state_ret,
           page_table, c_prompt, c_sample, norm1_w, norm2_w, w_ada, b_ada, w_in, fox_f_bias, fox_qn_w, fox_kn_w,
           gdn_conv_w, gdn_a_log, gdn_dt_bias, gdn_norm_w, w_branch, w_out, w_router, b_router,
           w_e1, b_e1, w_e2, b_e2):
    batch, seq, d = x_prompt.shape
    dec_batch, dec_seq, _ = x_sample.shape
    depth = w_in.shape[0]
    n_pool = cache_fox_k.shape[1]
    n_pages = page_table.shape[1]
    past_len = n_pages * PAGE_SIZE
    mp = batch * seq
    ms = dec_batch * dec_seq

    n_cond = batch + dec_batch
    c_all = jnp.concatenate([c_prompt, c_sample, jnp.zeros((-n_cond % SUBLANES, d), F32)], axis=0)
    mod_all = _modulation(c_all, w_ada, b_ada)

    cache_k = cache_fox_k.reshape(depth, n_pool, PAGE_SIZE * N_HEADS, HEAD_DIM)
    cache_v = cache_fox_v.reshape(depth, n_pool, PAGE_SIZE * N_HEADS, HEAD_DIM)
    logf_t = cache_fox_logf.transpose(0, 1, 3, 2)
    pt_flat = page_table.reshape(-1).astype(jnp.int32)

    xp = x_prompt.reshape(mp, d)
    xs = x_sample.reshape(ms, d)
    st_p, st_s = [], []
    for l in range(depth):
        lw = _layer_weights(l, d, w_in, fox_f_bias, gdn_a_log, gdn_dt_bias, w_router, b_router)
        p = dict(norm1=norm1_w[l], norm2=norm2_w[l], fox_qn=fox_qn_w[l], fox_kn=fox_kn_w[l],
                 gdn_conv=gdn_conv_w[l], gdn_nw=gdn_norm_w[l], w_branch=w_branch[l], w_out=w_out[l])
        mod_p = [t[:, None, :] for t in jnp.split(mod_all[l, :batch], 6, axis=-1)]
        mod_s = [t[None] for t in jnp.split(jnp.repeat(mod_all[l, batch:n_cond], dec_seq, axis=0), 6, axis=-1)]
        past = dict(pt=pt_flat, k=cache_k, v=cache_v, logf_t=logf_t, gdn=state_gdn[l],
                    conv=state_gdn_conv[l], ret=state_ret[l])
        x1p, h2p, sp = _prompt_mixer(xp, mod_p, lw, p, batch, seq)
        x1s, h2s, ss = _sample_mixer(xs, mod_s, lw, p, past, l, dec_batch, dec_seq, past_len)
        ys, dest, wt = _moe(jnp.concatenate([h2p, h2s], axis=0), lw, w_e1[l], b_e1[l], w_e2[l], b_e2[l])
        xp = _combine(dest[:mp * TOP_K], ys, wt[:mp], x1p, mod_p[5], LANES, seq)
        xs = _combine(dest[mp * TOP_K:], ys, wt[mp:], x1s, mod_s[5], LANES, ms)
        st_p.append(sp)
        st_s.append(ss)

    def stk(sts, i):
        return jnp.stack([s[i] for s in sts])

    return (xp.reshape(batch, seq, d), xs.reshape(dec_batch, dec_seq, d),
            stk(st_p, 0), stk(st_p, 1), stk(st_p, 2), stk(st_p, 3), stk(st_p, 4), stk(st_p, 5),
            stk(st_s, 0), stk(st_s, 1), stk(st_s, 2), stk(st_s, 3), stk(st_s, 4), stk(st_s, 5))
```

```python
import functools
import math

import jax
import jax.numpy as jnp
from jax import lax
from jax.experimental import pallas as pl
from jax.experimental.pallas import tpu as pltpu

F32 = jnp.float32
BF16 = jnp.bfloat16

HEAD_DIM = 128
N_HEADS = 8
MIX_W = N_HEADS * HEAD_DIM
N_BRANCH = 3
CONV_W = 4
CHUNK = 64
PAGE_SIZE = 128
ROPE_BASE = 10000.0
N_EXPERTS = 32
TOP_K = 4
SWIGLU_LIMIT = 7.0
SWIGLU_ALPHA = 1.702
EPS = 1e-6

LANES = 128
SUBLANES = 8
VMEM_LIMIT = 48 * 1024 * 1024
MOE_VMEM_LIMIT = 56 * 1024 * 1024
MOE_BLK = 256
MOE_TN = 1024
NEG_BIG = -1e30


def _params(sem, vmem=VMEM_LIMIT):
    return pltpu.CompilerParams(dimension_semantics=sem, vmem_limit_bytes=vmem)


def _dot(a, b):
    return jnp.dot(a, b, preferred_element_type=F32)


def _dot_nt(a, b):
    return lax.dot_general(a, b, (((1,), (1,)), ((), ())), preferred_element_type=F32)


def _dot_tn(a, b):
    return lax.dot_general(a, b, (((0,), (0,)), ((), ())), preferred_element_type=F32)


def _split2(a):
    hi = a.astype(BF16)
    lo = (a - hi.astype(F32)).astype(BF16)
    return hi, lo


def _split3(a):
    hi = a.astype(BF16)
    r = a - hi.astype(F32)
    mid = r.astype(BF16)
    lo = (r - mid.astype(F32)).astype(BF16)
    return hi, mid, lo


def _dot_exact_lhs(mask_bf16, x):
    hi, mid, lo = _split3(x)
    return _dot(mask_bf16, hi) + (_dot(mask_bf16, mid) + _dot(mask_bf16, lo))


def _dot_exact_rhs(x, mask_bf16):
    hi, mid, lo = _split3(x)
    return _dot(hi, mask_bf16) + (_dot(mid, mask_bf16) + _dot(lo, mask_bf16))


def _sigmoid(x):
    return 1.0 / (1.0 + jnp.exp(-x))


def _silu(x):
    return x * _sigmoid(x)


def _softplus(x):
    return jnp.maximum(x, 0.0) + jnp.log1p(jnp.exp(-jnp.abs(x)))


def _iota2(shape, dim):
    return lax.broadcasted_iota(jnp.int32, shape, dim)


def _modulation_kernel(c_ref, w_ref, b_ref, o_ref):
    c = _silu(c_ref[...])
    c_hi, c_lo = _split2(c)
    w_hi, w_lo = _split2(w_ref[...])
    acc = _dot(c_hi, w_hi) + (_dot(c_hi, w_lo) + _dot(c_lo, w_hi))
    o_ref[...] = acc + b_ref[...]


def _modulation(c_all, w_ada, b_ada, tn=1024):
    depth, d, n = w_ada.shape
    rows = c_all.shape[0]
    return pl.pallas_call(
        _modulation_kernel,
        out_shape=jax.ShapeDtypeStruct((depth, rows, n), F32),
        grid=(depth, n // tn),
        in_specs=[
            pl.BlockSpec((rows, d), lambda l, j: (0, 0)),
            pl.BlockSpec((None, d, tn), lambda l, j: (l, 0, j)),
            pl.BlockSpec((None, 1, tn), lambda l, j: (l, 0, j)),
        ],
        out_specs=pl.BlockSpec((None, rows, tn), lambda l, j: (l, 0, j)),
        compiler_params=_params(("arbitrary", "arbitrary")),
        name="adaln_modulation",
    )(c_all, w_ada, b_ada.reshape(depth, 1, n))


def _rms(x):
    return x * lax.rsqrt(jnp.mean(x * x, axis=-1, keepdims=True) + EPS)


def _norm_mod_kernel(x_ref, w_ref, sc_ref, sh_ref, o_ref):
    y = _rms(x_ref[...]) * w_ref[...]
    o_ref[...] = (y * (1.0 + sc_ref[...]) + sh_ref[...]).astype(o_ref.dtype)


def _mod_spec(mod, tm, rows_per_group):
    r, d = mod.shape[1], mod.shape[2]
    tiles = rows_per_group // tm
    return pl.BlockSpec((None, r, d), lambda i: (i // tiles, 0, 0))


def _norm_mod(x, w, sc, sh, tm, rows_per_group, out_dtype=BF16):
    m, d = x.shape
    return pl.pallas_call(
        _norm_mod_kernel,
        out_shape=jax.ShapeDtypeStruct((m, d), out_dtype),
        grid=(m // tm,),
        in_specs=[
            pl.BlockSpec((tm, d), lambda i: (i, 0)),
            pl.BlockSpec((1, d), lambda i: (0, 0)),
            _mod_spec(sc, tm, rows_per_group),
            _mod_spec(sh, tm, rows_per_group),
        ],
        out_specs=pl.BlockSpec((tm, d), lambda i: (i, 0)),
        compiler_params=_params(("arbitrary",)),
        name="norm_modulate",
    )(x, w.reshape(1, d), sc, sh)


def _mm_kernel(a_ref, w_ref, o_ref, w16_ref):
    @pl.when(pl.program_id(1) == 0)
    def _():
        w16_ref[...] = w_ref[...].astype(BF16)

    o_ref[...] = _dot(a_ref[...], w16_ref[...]).astype(o_ref.dtype)


def _mm_residual_kernel(a_ref, w_ref, res_ref, gate_ref, o_ref, w16_ref):
    @pl.when(pl.program_id(1) == 0)
    def _():
        w16_ref[...] = w_ref[...].astype(BF16)

    o_ref[...] = res_ref[...] + gate_ref[...] * _dot(a_ref[...], w16_ref[...])


def _mm(a, w, tm, tn, out_dtype=F32):
    m, k = a.shape
    n = w.shape[1]
    return pl.pallas_call(
        _mm_kernel,
        out_shape=jax.ShapeDtypeStruct((m, n), out_dtype),
        grid=(n // tn, m // tm),
        in_specs=[
            pl.BlockSpec((tm, k), lambda j, i: (i, 0)),
            pl.BlockSpec((k, tn), lambda j, i: (0, j)),
        ],
        out_specs=pl.BlockSpec((tm, tn), lambda j, i: (i, j)),
        scratch_shapes=[pltpu.VMEM((k, tn), BF16)],
        compiler_params=_params(("arbitrary", "arbitrary")),
        name="matmul",
    )(a, w)


def _mm_residual(a, w, layer, res, gate, tm, tn, rows_per_group):
    m, k = a.shape
    n = w.shape[2]
    r = gate.shape[1]
    tiles = rows_per_group // tm
    return pl.pallas_call(
        _mm_residual_kernel,
        out_shape=jax.ShapeDtypeStruct((m, n), F32),
        grid=(n // tn, m // tm),
        in_specs=[
            pl.BlockSpec((tm, k), lambda j, i: (i, 0)),
            pl.BlockSpec((None, k, tn), lambda j, i: (layer, 0, j)),
            pl.BlockSpec((tm, tn), lambda j, i: (i, j)),
            pl.BlockSpec((None, r, tn), lambda j, i: (i // tiles, 0, j)),
        ],
        out_specs=pl.BlockSpec((tm, tn), lambda j, i: (i, j)),
        scratch_shapes=[pltpu.VMEM((k, tn), BF16)],
        compiler_params=_params(("arbitrary", "arbitrary")),
        name="matmul_residual",
    )(a, w, res, gate)


def _gates_kernel(h_ref, whi_ref, wlo_ref, prm_ref, o_ref):
    h = h_ref[...]
    z = _dot(h, whi_ref[...]) + _dot(h, wlo_ref[...]) + prm_ref[0:1, :]
    lane = _iota2(z.shape, 1)
    tail = jnp.log1p(jnp.exp(-jnp.abs(z)))
    lf = jnp.minimum(z, 0.0) - tail
    g = prm_ref[1:2, :] * (jnp.maximum(z, 0.0) + tail)
    beta = _sigmoid(z)
    o_ref[...] = jnp.where(lane < 8, lf, jnp.where(lane < 16, g, beta))


def _gates(h, w_hi, w_lo, prm, tm):
    m, d = h.shape
    return pl.pallas_call(
        _gates_kernel,
        out_shape=jax.ShapeDtypeStruct((m, LANES), F32),
        grid=(m // tm,),
        in_specs=[
            pl.BlockSpec((tm, d), lambda i: (i, 0)),
            pl.BlockSpec((d, LANES), lambda i: (0, 0)),
            pl.BlockSpec((d, LANES), lambda i: (0, 0)),
            pl.BlockSpec((SUBLANES, LANES), lambda i: (0, 0)),
        ],
        out_specs=pl.BlockSpec((tm, LANES), lambda i: (i, 0)),
        compiler_params=_params(("arbitrary",)),
        name="gate_projection",
    )(h, w_hi, w_lo, prm)


def _cumsum_kernel(x_ref, o_ref, carry_ref, *, group):
    tm = x_ref.shape[0]
    r = _iota2((tm, tm), 0)
    c = _iota2((tm, tm), 1)
    if group >= tm:
        tri = jnp.where(c <= r, 1.0, 0.0).astype(BF16)
    else:
        tri = jnp.where(c <= r, jnp.where(r // group == c // group, 1.0, 0.0), 0.0).astype(BF16)
    cs = _dot_exact_lhs(tri, x_ref[...])
    if group > tm:
        @pl.when(pl.program_id(0) % (group // tm) == 0)
        def _():
            carry_ref[...] = jnp.zeros_like(carry_ref)

        cs = cs + carry_ref[...]
        carry_ref[...] = cs[tm - 1:tm, :]
    o_ref[...] = cs


def _cumsum_rows(x, tm, group):
    m, n = x.shape
    return pl.pallas_call(
        functools.partial(_cumsum_kernel, group=group),
        out_shape=jax.ShapeDtypeStruct((m, n), F32),
        grid=(m // tm,),
        in_specs=[pl.BlockSpec((tm, n), lambda i: (i, 0))],
        out_specs=pl.BlockSpec((tm, n), lambda i: (i, 0)),
        scratch_shapes=[pltpu.VMEM((1, n), F32)],
        compiler_params=_params(("arbitrary",)),
        name="cumsum_rows",
    )(x)


def _fox_prep_kernel(q_ref, k_ref, v_ref, qw_ref, kw_ref, fq16_ref, fk_ref, fk16_ref, fv16_ref):
    for h in range(N_HEADS):
        sl = slice(h * HEAD_DIM, (h + 1) * HEAD_DIM)
        fq16_ref[:, sl] = (_rms(q_ref[:, sl]) * qw_ref[...]).astype(BF16)
        kn = _rms(k_ref[:, sl]) * kw_ref[...]
        fk_ref[:, sl] = kn
        fk16_ref[:, sl] = kn.astype(BF16)
    fv16_ref[...] = v_ref[...].astype(BF16)


def _fox_prep(qkv, qw, kw, tm):
    m = qkv.shape[0]
    col = lambda c: pl.BlockSpec((tm, MIX_W), lambda i: (i, c))
    wspec = pl.BlockSpec((1, HEAD_DIM), lambda i: (0, 0))
    return pl.pallas_call(
        _fox_prep_kernel,
        out_shape=(
            jax.ShapeDtypeStruct((m, MIX_W), BF16),
            jax.ShapeDtypeStruct((m, MIX_W), F32),
            jax.ShapeDtypeStruct((m, MIX_W), BF16),
            jax.ShapeDtypeStruct((m, MIX_W), BF16),
        ),
        grid=(m // tm,),
        in_specs=[col(0), col(1), col(2), wspec, wspec],
        out_specs=(col(0), col(0), col(0), col(0)),
        compiler_params=_params(("arbitrary",)),
        name="fox_qk_norm",
    )(qkv, qkv, qkv, qw.reshape(1, HEAD_DIM), kw.reshape(1, HEAD_DIM))


def _online_softmax_update(s, v16, m_ref, l_ref, acc_ref):
    m_prev = m_ref[...]
    m_new = jnp.maximum(m_prev, jnp.max(s, axis=1, keepdims=True))
    alpha = jnp.exp(m_prev - m_new)
    p = jnp.exp(s - m_new)
    l_ref[...] = alpha * l_ref[...] + jnp.sum(p, axis=1, keepdims=True)
    acc_ref[...] = alpha * acc_ref[...] + _dot(p.astype(BF16), v16)
    m_ref[...] = m_new


def _flash_kernel(qi_ref, kj_ref, q_ref, k_ref, v_ref, cq_ref, ck_ref, o_ref, m_ref, l_ref, acc_ref, *, scale):
    i = qi_ref[pl.program_id(2)]
    j = kj_ref[pl.program_id(2)]

    @pl.when(j == 0)
    def _():
        m_ref[...] = jnp.full_like(m_ref, NEG_BIG)
        l_ref[...] = jnp.zeros_like(l_ref)
        acc_ref[...] = jnp.zeros_like(acc_ref)

    def scores():
        return _dot_nt(q_ref[...], k_ref[...]) * scale + (cq_ref[...] - ck_ref[...])

    @pl.when(j < i)
    def _():
        _online_softmax_update(scores(), v_ref[...], m_ref, l_ref, acc_ref)

    @pl.when(j == i)
    def _():
        s = scores()
        s = jnp.where(_iota2(s.shape, 0) >= _iota2(s.shape, 1), s, NEG_BIG)
        _online_softmax_update(s, v_ref[...], m_ref, l_ref, acc_ref)
        o_ref[...] = (acc_ref[...] / l_ref[...]).astype(o_ref.dtype)


def _flash_attention(fq16, fk16, fv16, cum_col, cum_row, batch, seq, t):
    n = seq // t
    pairs = [(i, j) for i in range(n) for j in range(i + 1)]
    qi = jnp.asarray([i for i, _ in pairs], jnp.int32)
    kj = jnp.asarray([j for _, j in pairs], jnp.int32)
    qspec = pl.BlockSpec((t, HEAD_DIM), lambda b, h, s, qi, kj: (b * n + qi[s], h))
    kspec = pl.BlockSpec((t, HEAD_DIM), lambda b, h, s, qi, kj: (b * n + kj[s], h))
    return pl.pallas_call(
        functools.partial(_flash_kernel, scale=HEAD_DIM ** -0.5),
        out_shape=jax.ShapeDtypeStruct((batch * seq, MIX_W), BF16),
        grid_spec=pltpu.PrefetchScalarGridSpec(
            num_scalar_prefetch=2,
            grid=(batch, N_HEADS, len(pairs)),
            in_specs=[
                qspec, kspec, kspec,
                pl.BlockSpec((None, None, t, 1), lambda b, h, s, qi, kj: (b, h, qi[s], 0)),
                pl.BlockSpec((None, None, 1, t), lambda b, h, s, qi, kj: (b, h, 0, kj[s])),
            ],
            out_specs=qspec,
            scratch_shapes=[pltpu.VMEM((t, 1), F32), pltpu.VMEM((t, 1), F32), pltpu.VMEM((t, HEAD_DIM), F32)],
        ),
        compiler_params=_params(("arbitrary", "arbitrary", "arbitrary")),
        name="fox_flash_attention",
    )(qi, kj, fq16, fk16, fv16, cum_col, cum_row)


def _suffix_kernel(pt_ref, x_ref, o_ref, xs_ref, *, n_pages):
    b = pl.program_id(0)
    for p in range(n_pages):
        xs_ref[p * N_HEADS:(p + 1) * N_HEADS, :] = x_ref[pt_ref[b * n_pages + p]]
    x = xs_ref[...]
    rows, t = x.shape
    later = jnp.where(_iota2((t, t), 0) > _iota2((t, t), 1), 1.0, 0.0).astype(BF16)
    r = _iota2((rows, rows), 0)
    col = _iota2((rows, rows), 1)
    later_pages = jnp.where(col // N_HEADS > r // N_HEADS,
                            jnp.where(col % N_HEADS == r % N_HEADS, 1.0, 0.0), 0.0).astype(BF16)
    totals = jnp.broadcast_to(jnp.sum(x, axis=1, keepdims=True), x.shape)
    o_ref[...] = _dot_exact_rhs(x, later) + _dot_exact_lhs(later_pages, totals)


def _suffix_sums(page_table_flat, logf_t, layer, n_batch, n_pages):
    n_pool = logf_t.shape[1]
    rows = n_pages * N_HEADS
    return pl.pallas_call(
        functools.partial(_suffix_kernel, n_pages=n_pages),
        out_shape=jax.ShapeDtypeStruct((n_batch, rows, PAGE_SIZE), F32),
        grid_spec=pltpu.PrefetchScalarGridSpec(
            num_scalar_prefetch=1,
            grid=(n_batch,),
            in_specs=[pl.BlockSpec((None, n_pool, N_HEADS, PAGE_SIZE), lambda b, pt: (layer, 0, 0, 0))],
            out_specs=pl.BlockSpec((None, rows, PAGE_SIZE), lambda b, pt: (b, 0, 0)),
            scratch_shapes=[pltpu.VMEM((rows, PAGE_SIZE), F32)],
        ),
        compiler_params=_params(("arbitrary",)),
        name="fox_past_suffix",
    )(page_table_flat, logf_t)


DECODE_PAGES = 8


def _decode_kernel(pt_ref, q_ref, *refs, scale, npg):
    k_refs, v_refs = refs[:npg], refs[npg:2 * npg]
    bias_ref, cq_ref, kn_ref, vn_ref, ckn_ref, o_ref, m_ref, l_ref, acc_ref = refs[2 * npg:]
    p = pl.program_id(1)

    @pl.when(p == 0)
    def _():
        m_ref[...] = jnp.full_like(m_ref, NEG_BIG)
        l_ref[...] = jnp.zeros_like(l_ref)
        acc_ref[...] = jnp.zeros_like(acc_ref)

    def head_rows(h):
        return slice(h * SUBLANES, (h + 1) * SUBLANES)

    def head_of_page(ref, h):
        return ref[pl.ds(h, PAGE_SIZE, stride=N_HEADS), :].astype(BF16)

    heads = range(N_HEADS)
    q16s = [q_ref[head_rows(h), :].astype(BF16) for h in heads]
    cqs = [cq_ref[head_rows(h), :] for h in heads]
    bias = bias_ref[...]
    ss = [jnp.concatenate(
        [_dot_nt(q16s[h], head_of_page(k_refs[i], h)) * scale
         + (cqs[h] + bias[i * N_HEADS + h:i * N_HEADS + h + 1, :]) for i in range(npg)], axis=1) for h in heads]
    m_prev = m_ref[...]
    m_new = jnp.maximum(m_prev, jnp.concatenate([jnp.max(s, axis=1, keepdims=True) for s in ss], axis=0))
    alpha = jnp.exp(m_prev - m_new)
    ps = [jnp.exp(ss[h] - m_new[head_rows(h), :]) for h in heads]
    l_ref[...] = alpha * l_ref[...] + jnp.concatenate([jnp.sum(p, axis=1, keepdims=True) for p in ps], axis=0)
    pv = jnp.concatenate(
        [_dot(ps[h].astype(BF16), jnp.concatenate([head_of_page(v_refs[i], h) for i in range(npg)], axis=0))
         for h in heads], axis=0)
    acc_ref[...] = alpha * acc_ref[...] + pv
    m_ref[...] = m_new

    @pl.when(p == pl.num_programs(1) - 1)
    def _():
        for h in range(N_HEADS):
            rows = head_rows(h)
            new = slice(h * LANES, (h + 1) * LANES)
            s2 = _dot_nt(q16s[h], kn_ref[new, :]) * scale + (cq_ref[rows, :] - ckn_ref[h:h + 1, :])
            s2 = jnp.where(_iota2(s2.shape, 1) <= _iota2(s2.shape, 0), s2, NEG_BIG)
            _online_softmax_update(s2, vn_ref[new, :], m_ref.at[rows], l_ref.at[rows], acc_ref.at[rows])
        o_ref[...] = acc_ref[...] / l_ref[...]


def _decode_attention(page_table_flat, q64, cache_k, cache_v, bias, cq, kn, vn, ckn, layer, n_batch, n_pages):
    rows = N_HEADS * SUBLANES
    flat = PAGE_SIZE * N_HEADS
    npg = DECODE_PAGES
    page = lambda i: pl.BlockSpec((None, None, flat, HEAD_DIM),
                                  lambda b, p, pt: (layer, pt[b * n_pages + p * npg + i], 0, 0))
    per_b = lambda shape: pl.BlockSpec((None,) + shape, lambda b, p, pt: (b, 0, 0))
    return pl.pallas_call(
        functools.partial(_decode_kernel, scale=HEAD_DIM ** -0.5, npg=npg),
        out_shape=jax.ShapeDtypeStruct((n_batch, rows, HEAD_DIM), F32),
        grid_spec=pltpu.PrefetchScalarGridSpec(
            num_scalar_prefetch=1,
            grid=(n_batch, n_pages // npg),
            in_specs=[per_b((rows, HEAD_DIM))] + [page(i) for i in range(npg)] + [page(i) for i in range(npg)] + [
                pl.BlockSpec((None, npg * N_HEADS, PAGE_SIZE), lambda b, p, pt: (b, p, 0)),
                per_b((rows, 1)),
                per_b((N_HEADS * LANES, HEAD_DIM)), per_b((N_HEADS * LANES, HEAD_DIM)), per_b((N_HEADS, LANES)),
            ],
            out_specs=per_b((rows, HEAD_DIM)),
            scratch_shapes=[pltpu.VMEM((rows, 1), F32), pltpu.VMEM((rows, 1), F32),
                            pltpu.VMEM((rows, HEAD_DIM), F32)],
        ),
        compiler_params=_params(("arbitrary", "arbitrary")),
        name="fox_paged_decode",
    )(page_table_flat, q64, *([cache_k] * npg), *([cache_v] * npg), bias, cq, kn, vn, ckn)


def _unit_lower_inverses(ms, c):
    r = _iota2((c, c), 0)
    col = _iota2((c, c), 1)
    eye = jnp.where(r == col, 1.0, 0.0)
    pair = r // 2 == col // 2
    invs = [eye - jnp.where(pair, m, 0.0) for m in ms]
    s = 2
    while s < c:
        same_block = r // (2 * s) == col // (2 * s)
        other_half = r // s != col // s
        offs = [jnp.where(same_block, jnp.where(other_half, m, 0.0), 0.0).astype(BF16) for m in ms]
        inv16s = [inv.astype(BF16) for inv in invs]
        mids = [_dot(off, inv16).astype(BF16) for off, inv16 in zip(offs, inv16s)]
        invs = [inv - _dot(inv16, mid) for inv, inv16, mid in zip(invs, inv16s, mids)]
        s *= 2
    return invs


def _gdn_kernel(qkv_ref, convw_ref, conv0_ref, gates_ref, gates_t_ref, z_ref, nw_ref, s0_ref,
                o_ref, st_ref, ext_ref, s_ref, *, c):
    n = pl.program_id(1)

    @pl.when(n == 0)
    def _():
        ext_ref[0:SUBLANES, :] = conv0_ref[...]
        s_ref[...] = s0_ref[...]

    ext_ref[SUBLANES:SUBLANES + c, :] = qkv_ref[...]

    def conv(c0):
        acc = ext_ref[SUBLANES:SUBLANES + c, c0:c0 + HEAD_DIM] * convw_ref[CONV_W - 1:CONV_W, c0:c0 + HEAD_DIM]
        for i in range(1, CONV_W):
            acc = acc + (ext_ref[SUBLANES - i:SUBLANES - i + c, c0:c0 + HEAD_DIM]
                         * convw_ref[CONV_W - 1 - i:CONV_W - i, c0:c0 + HEAD_DIM])
        return _silu(acc)

    def l2n(x):
        return x * lax.rsqrt(jnp.sum(x * x, axis=-1, keepdims=True) + EPS)

    r = _iota2((c, c), 0)
    col = _iota2((c, c), 1)
    gates = gates_ref[...]
    incl_lower = jnp.where(col <= r, 1.0, 0.0).astype(BF16)
    incl_upper = jnp.where(r <= col, 1.0, 0.0).astype(BF16)
    gc_cols = _dot_exact_lhs(incl_lower, gates)
    gc_rows = _dot_exact_rhs(gates_t_ref[...], incl_upper)

    heads = range(N_HEADS)
    qs = [l2n(conv(h * HEAD_DIM)) * (HEAD_DIM ** -0.5) for h in heads]
    ks = [l2n(conv(MIX_W + h * HEAD_DIM)) for h in heads]
    vs = [conv(2 * MIX_W + h * HEAD_DIM) for h in heads]
    ext_ref[0:SUBLANES, :] = ext_ref[c:c + SUBLANES, :]
    states = [s_ref[h] for h in heads]
    betas = [gates[:, 16 + h:17 + h] for h in heads]
    gcs = [gc_cols[:, 8 + h:9 + h] for h in heads]
    gls = [gc[c - 1:c, :] for gc in gcs]
    decays = [jnp.exp(jnp.where(col <= r, gcs[h] - gc_rows[8 + h:9 + h, :], NEG_BIG)) for h in heads]
    kbs = [ks[h] * betas[h] for h in heads]
    k16s = [k.astype(BF16) for k in ks]
    ms = [jnp.where(col < r, _dot_nt(kbs[h].astype(BF16), k16s[h]) * decays[h], 0.0) for h in heads]
    a16s = [(_dot_nt(qs[h].astype(BF16), k16s[h]) * decays[h]).astype(BF16) for h in heads]
    t16s = [t.astype(BF16) for t in _unit_lower_inverses(ms, c)]
    u0s = [_dot(t16s[h], (vs[h] * betas[h]).astype(BF16)) for h in heads]
    wk16s = [_dot(t16s[h], (kbs[h] * jnp.exp(gcs[h])).astype(BF16)).astype(BF16) for h in heads]
    s16s = [s.astype(BF16) for s in states]
    u16s = [(u0s[h] - _dot(wk16s[h], s16s[h])).astype(BF16) for h in heads]
    outs = [_dot((qs[h] * jnp.exp(gcs[h])).astype(BF16), s16s[h]) + _dot(a16s[h], u16s[h]) for h in heads]
    new_states = [states[h] * jnp.exp(gls[h]) + _dot_tn((ks[h] * jnp.exp(gls[h] - gcs[h])).astype(BF16), u16s[h])
                  for h in heads]
    for h in heads:
        s_ref[h] = new_states[h]
        zh = z_ref[:, h * HEAD_DIM:(h + 1) * HEAD_DIM]
        o_ref[:, h * HEAD_DIM:(h + 1) * HEAD_DIM] = (_rms(outs[h]) * nw_ref[...] * _silu(zh)).astype(o_ref.dtype)

    @pl.when(n == pl.num_programs(1) - 1)
    def _():
        st_ref[...] = s_ref[...]


def _gdn(qkv, conv_w, conv0, gates, gates_t, z, norm_w, s0, batch, n_chunks, c, out_dtype):
    m = qkv.shape[0]
    w3 = 3 * MIX_W
    row = lambda width: pl.BlockSpec((c, width), lambda b, n: (b * n_chunks + n, 0))
    state = pl.BlockSpec((None, N_HEADS, HEAD_DIM, HEAD_DIM), lambda b, n: (b, 0, 0, 0))
    return pl.pallas_call(
        functools.partial(_gdn_kernel, c=c),
        out_shape=(jax.ShapeDtypeStruct((m, MIX_W), out_dtype),
                   jax.ShapeDtypeStruct((batch, N_HEADS, HEAD_DIM, HEAD_DIM), F32)),
        grid=(batch, n_chunks),
        in_specs=[
            row(w3),
            pl.BlockSpec((CONV_W, w3), lambda b, n: (0, 0)),
            pl.BlockSpec((None, SUBLANES, w3), lambda b, n: (b, 0, 0)),
            row(LANES),
            pl.BlockSpec((None, LANES, c), lambda b, n: (b * n_chunks + n, 0, 0)),
            row(MIX_W),
            pl.BlockSpec((1, HEAD_DIM), lambda b, n: (0, 0)),
            state,
        ],
        out_specs=(row(MIX_W), state),
        scratch_shapes=[pltpu.VMEM((c + SUBLANES, w3), F32),
                        pltpu.VMEM((N_HEADS, HEAD_DIM, HEAD_DIM), F32)],
        compiler_params=_params(("arbitrary", "arbitrary")),
        name="gated_deltanet",
    )(qkv, conv_w, conv0, gates, gates_t, z, norm_w.reshape(1, HEAD_DIM), s0)


def _ret_kernel(q_ref, k_ref, v_ref, g_ref, cos_ref, sin_ref, s0_ref, o_ref, st_ref, s_ref, *, c, n_valid):
    n = pl.program_id(1)

    @pl.when(n == 0)
    def _():
        s_ref[...] = s0_ref[...]

    cos = cos_ref[...]
    sin = sin_ref[...]

    def rot(x):
        return x * cos + pltpu.roll(x, HEAD_DIM // 2, 1) * sin

    r = _iota2((c, c), 0)
    col = _iota2((c, c), 1)
    cnt_r = jnp.minimum(r + 1, n_valid).astype(F32)
    cnt_c = jnp.minimum(col + 1, n_valid).astype(F32)
    cnt = cnt_r[:, 0:1]
    total = float(min(c, n_valid))

    heads = range(N_HEADS)
    lgs = [math.log(1.0 - 2.0 ** (-5.0 - h)) for h in heads]
    sls = [slice(h * HEAD_DIM, (h + 1) * HEAD_DIM) for h in heads]
    qs = [rot(q_ref[:, sl]) for sl in sls]
    ks = [rot(k_ref[:, sl]) * (HEAD_DIM ** -0.5) for sl in sls]
    v16s = [v_ref[:, sl].astype(BF16) for sl in sls]
    states = [s_ref[h] for h in heads]
    a16s = [(_dot_nt(qs[h].astype(BF16), ks[h].astype(BF16))
             * jnp.exp(jnp.where(col <= r, (cnt_r - cnt_c) * lgs[h], NEG_BIG))).astype(BF16) for h in heads]
    outs = [_dot(a16s[h], v16s[h]) + _dot((qs[h] * jnp.exp(cnt * lgs[h])).astype(BF16), states[h].astype(BF16))
            for h in heads]
    new_states = [states[h] * math.exp(total * lgs[h])
                  + _dot_tn((ks[h] * jnp.exp((total - cnt) * lgs[h])).astype(BF16), v16s[h]) for h in heads]
    for h in heads:
        s_ref[h] = new_states[h]
        o_ref[:, sls[h]] = (_rms(outs[h]) * _silu(g_ref[:, sls[h]])).astype(o_ref.dtype)

    @pl.when(n == pl.num_programs(1) - 1)
    def _():
        st_ref[...] = s_ref[...]


def _retention(qkvg, cos, sin, s0, batch, n_chunks, c, n_valid, out_dtype):
    m = qkvg.shape[0]
    colblk = lambda j: pl.BlockSpec((c, MIX_W), lambda b, n: (b * n_chunks + n, j))
    table = pl.BlockSpec((c, HEAD_DIM), lambda b, n: (n, 0))
    state = pl.BlockSpec((None, N_HEADS, HEAD_DIM, HEAD_DIM), lambda b, n: (b, 0, 0, 0))
    return pl.pallas_call(
        functools.partial(_ret_kernel, c=c, n_valid=n_valid),
        out_shape=(jax.ShapeDtypeStruct((m, MIX_W), out_dtype),
                   jax.ShapeDtypeStruct((batch, N_HEADS, HEAD_DIM, HEAD_DIM), F32)),
        grid=(batch, n_chunks),
        in_specs=[colblk(0), colblk(1), colblk(2), colblk(3), table, table, state],
        out_specs=(colblk(0), state),
        scratch_shapes=[pltpu.VMEM((N_HEADS, HEAD_DIM, HEAD_DIM), F32)],
        compiler_params=_params(("arbitrary", "arbitrary")),
        name="retention",
    )(qkvg, qkvg, qkvg, qkvg, cos, sin, s0)


def _merge_kernel(fox_ref, gdn_ref, ret_ref, w_ref, g0_ref, g1_ref, g2_ref, o_ref, w16_ref):
    @pl.when(pl.program_id(1) == 0)
    def _():
        w16_ref[...] = w_ref[...].astype(BF16)

    acc = _sigmoid(g0_ref[...]) * _dot(fox_ref[...], w16_ref[0])
    acc = acc + _sigmoid(g1_ref[...]) * _dot(gdn_ref[...], w16_ref[1])
    acc = acc + _sigmoid(g2_ref[...]) * _dot(ret_ref[...], w16_ref[2])
    o_ref[...] = acc.astype(o_ref.dtype)


def _merge(o_fox, o_gdn, o_ret, w_branch, layer, gate, tm, tn):
    m = o_fox.shape[0]
    d = w_branch.shape[3]
    nj = d // tn
    br = pl.BlockSpec((tm, MIX_W), lambda j, i: (i, 0))
    gspec = lambda b: pl.BlockSpec((tm, tn), lambda j, i: (i, b * nj + j))
    return pl.pallas_call(
        _merge_kernel,
        out_shape=jax.ShapeDtypeStruct((m, d), BF16),
        grid=(nj, m // tm),
        in_specs=[br, br, br,
                  pl.BlockSpec((None, N_BRANCH, MIX_W, tn), lambda j, i: (layer, 0, 0, j)),
                  gspec(0), gspec(1), gspec(2)],
        out_specs=pl.BlockSpec((tm, tn), lambda j, i: (i, j)),
        scratch_shapes=[pltpu.VMEM((N_BRANCH, MIX_W, tn), BF16)],
        compiler_params=_params(("arbitrary", "arbitrary")),
        name="branch_merge",
    )(o_fox, o_gdn, o_ret, w_branch, gate, gate, gate)


def _router_kernel(h_ref, whi_ref, wlo_ref, b_ref, idx_ref, wt_ref, cnt_ref, carry_ref):
    @pl.when(pl.program_id(0) == 0)
    def _():
        carry_ref[...] = jnp.zeros_like(carry_ref)

    h_hi, h_lo = _split2(h_ref[...])
    logits = _dot(h_hi, whi_ref[...]) + (_dot(h_hi, wlo_ref[...]) + _dot(h_lo, whi_ref[...])) + b_ref[...]
    tm = logits.shape[0]
    lane = _iota2(logits.shape, 1)
    lane_f = lane.astype(F32)
    logits = jnp.where(lane < N_EXPERTS, logits, -jnp.inf)
    vals, sels = [], []
    for k in range(TOP_K):
        top = jnp.max(logits, axis=1, keepdims=True)
        sel = jnp.min(jnp.where(logits == top, lane_f, float(LANES)), axis=1, keepdims=True)
        vals.append(top)
        sels.append(sel)
        logits = jnp.where(lane_f == sel, -jnp.inf, logits)
    chosen = [jnp.where(lane_f == sel, 1.0, 0.0) for sel in sels]
    picks = chosen[0] + chosen[1] + chosen[2] + chosen[3]
    earlier_rows = jnp.where(_iota2((tm, tm), 1) < _iota2((tm, tm), 0), 1.0, 0.0).astype(BF16)
    before = _dot(earlier_rows, picks.astype(BF16)) + carry_ref[0:1, :]
    carry_ref[0:1, :] = carry_ref[0:1, :] + jnp.sum(picks, axis=0, keepdims=True)
    es = [jnp.exp(v - vals[0]) for v in vals]
    tot = es[0] + es[1] + es[2] + es[3]
    idx = jnp.zeros(logits.shape, F32)
    wt = jnp.zeros(logits.shape, F32)
    for k in range(TOP_K):
        rank = jnp.sum(before * chosen[k], axis=1, keepdims=True)
        idx = jnp.where(lane == k, sels[k], jnp.where(lane == TOP_K + k, rank, idx))
        wt = jnp.where(lane == k, es[k] / tot, wt)
    idx_ref[...] = idx.astype(jnp.int32)
    wt_ref[...] = wt
    cnt_ref[...] = carry_ref[...]


def _router(h, m, w_hi, w_lo, b, tm):
    d = h.shape[1]
    tok = pl.BlockSpec((tm, LANES), lambda i: (i, 0))
    const = lambda rows: pl.BlockSpec((rows, LANES), lambda i: (0, 0))
    return pl.pallas_call(
        _router_kernel,
        out_shape=(jax.ShapeDtypeStruct((m, LANES), jnp.int32), jax.ShapeDtypeStruct((m, LANES), F32),
                   jax.ShapeDtypeStruct((SUBLANES, LANES), F32)),
        grid=(m // tm,),
        in_specs=[pl.BlockSpec((tm, d), lambda i: (i, 0)), const(d), const(d), const(1)],
        out_specs=(tok, tok, const(SUBLANES)),
        scratch_shapes=[pltpu.VMEM((SUBLANES, LANES), F32)],
        compiler_params=_params(("arbitrary",)),
        name="moe_router_topk",
    )(h, w_hi, w_lo, b)


def _row_copy(src_hbm, src_row, dst_vmem, dst_row, sem):
    return pltpu.make_async_copy(src_hbm.at[pl.ds(src_row, 1), :], dst_vmem.at[pl.ds(dst_row, 1), :], sem)


def _gather_kernel(tok_ref, nv_ref, h_ref, o_ref, buf_ref, sems):
    b = pl.program_id(0)
    blk = buf_ref.shape[1]
    slot = b % 2

    def fetch(block, to_slot):
        def issue(r, carry):
            _row_copy(h_ref, tok_ref[block * blk + r], buf_ref.at[to_slot], r, sems.at[to_slot]).start()
            return carry

        lax.fori_loop(0, blk, issue, 0, unroll=8)

    @pl.when(b == 0)
    def _():
        fetch(0, 0)

    @pl.when(b + 1 < nv_ref[0])
    def _():
        fetch(b + 1, 1 - slot)

    @pl.when(b < nv_ref[0])
    def _():
        def wait(r, carry):
            _row_copy(h_ref, 0, buf_ref.at[slot], r, sems.at[slot]).wait()
            return carry

        lax.fori_loop(0, blk, wait, 0, unroll=8)
        o_ref[...] = buf_ref[slot].astype(o_ref.dtype)

    @pl.when(b >= nv_ref[0])
    def _():
        o_ref[...] = jnp.zeros_like(o_ref)


def _gather_rows(slot_tok, n_valid, h_ext, n_blk, blk):
    d = h_ext.shape[1]
    return pl.pallas_call(
        _gather_kernel,
        out_shape=jax.ShapeDtypeStruct((n_blk * blk, d), BF16),
        grid_spec=pltpu.PrefetchScalarGridSpec(
            num_scalar_prefetch=2,
            grid=(n_blk,),
            in_specs=[pl.BlockSpec(memory_space=pl.ANY)],
            out_specs=pl.BlockSpec((blk, d), lambda b, tok, nv: (b, 0)),
            scratch_shapes=[pltpu.VMEM((2, blk, d), F32), pltpu.SemaphoreType.DMA((2,))],
        ),
        compiler_params=_params(("arbitrary",)),
        name="moe_token_gather",
    )(slot_tok, n_valid, h_ext)


def _expert_changed(be_ref, b):
    return jnp.logical_or(b == 0, be_ref[b] != be_ref[jnp.maximum(b - 1, 0)])


def _gmm1_kernel(be_ref, nv_ref, x_ref, wg_ref, wl_ref, bg_ref, bl_ref, o_ref, wg16_ref, wl16_ref):
    b = pl.program_id(1)

    @pl.when(_expert_changed(be_ref, b))
    def _():
        wg16_ref[...] = wg_ref[...].astype(BF16)
        wl16_ref[...] = wl_ref[...].astype(BF16)

    @pl.when(b < nv_ref[0])
    def _():
        x = x_ref[...]
        glu = jnp.minimum(_dot(x, wg16_ref[...]) + bg_ref[...], SWIGLU_LIMIT)
        lin = jnp.clip(_dot(x, wl16_ref[...]) + bl_ref[...], -SWIGLU_LIMIT, SWIGLU_LIMIT)
        o_ref[...] = (glu * _sigmoid(SWIGLU_ALPHA * glu) * (lin + 1.0)).astype(o_ref.dtype)

    @pl.when(b >= nv_ref[0])
    def _():
        o_ref[...] = jnp.zeros_like(o_ref)


def _gmm1(blk_e, n_valid, xs, w_e1, b_e1, layer, n_blk, blk, tn):
    depth, n_exp, d, two_f = w_e1.shape
    d_ff = two_f // 2
    nj = d_ff // tn
    rowblk = lambda width: pl.BlockSpec((blk, width), lambda j, b, be, nv: (jnp.minimum(b, nv[0] - 1), 0))
    wspec = lambda rows, off: pl.BlockSpec((None, None, rows, tn), lambda j, b, be, nv: (layer, be[b], 0, off + j))
    bias = b_e1.reshape(depth, n_exp, 1, two_f)
    return pl.pallas_call(
        _gmm1_kernel,
        out_shape=jax.ShapeDtypeStruct((n_blk * blk, d_ff), BF16),
        grid_spec=pltpu.PrefetchScalarGridSpec(
            num_scalar_prefetch=2,
            grid=(nj, n_blk),
            in_specs=[rowblk(d), wspec(d, 0), wspec(d, nj), wspec(1, 0), wspec(1, nj)],
            out_specs=pl.BlockSpec((blk, tn), lambda j, b, be, nv: (b, j)),
            scratch_shapes=[pltpu.VMEM((d, tn), BF16), pltpu.VMEM((d, tn), BF16)],
        ),
        compiler_params=_params(("arbitrary", "arbitrary"), MOE_VMEM_LIMIT),
        name="moe_expert_up",
    )(blk_e, n_valid, xs, w_e1, w_e1, bias, bias)


def _gmm2_kernel(be_ref, nv_ref, a_ref, w_ref, b_ref, o_ref, w16_ref):
    b = pl.program_id(1)

    @pl.when(_expert_changed(be_ref, b))
    def _():
        w16_ref[...] = w_ref[...].astype(BF16)

    @pl.when(b < nv_ref[0])
    def _():
        o_ref[...] = _dot(a_ref[...], w16_ref[...]) + b_ref[...]

    @pl.when(b >= nv_ref[0])
    def _():
        o_ref[...] = jnp.zeros_like(o_ref)


def _gmm2(blk_e, n_valid, act, w_e2, b_e2, layer, n_blk, blk, tn):
    depth, n_exp, d_ff, d = w_e2.shape
    wspec = lambda rows: pl.BlockSpec((None, None, rows, tn), lambda j, b, be, nv: (layer, be[b], 0, j))
    return pl.pallas_call(
        _gmm2_kernel,
        out_shape=jax.ShapeDtypeStruct((n_blk * blk, d), F32),
        grid_spec=pltpu.PrefetchScalarGridSpec(
            num_scalar_prefetch=2,
            grid=(d // tn, n_blk),
            in_specs=[
                pl.BlockSpec((blk, d_ff), lambda j, b, be, nv: (jnp.minimum(b, nv[0] - 1), 0)),
                wspec(d_ff), wspec(1),
            ],
            out_specs=pl.BlockSpec((blk, tn), lambda j, b, be, nv: (b, j)),
            scratch_shapes=[pltpu.VMEM((d_ff, tn), BF16)],
        ),
        compiler_params=_params(("arbitrary", "arbitrary"), MOE_VMEM_LIMIT),
        name="moe_expert_down",
    )(blk_e, n_valid, act, w_e2, b_e2.reshape(depth, n_exp, 1, d))


def _combine_kernel(pos_ref, ys_ref, wt_ref, x_ref, gate_ref, o_ref, buf_ref, sem):
    tm = x_ref.shape[0]
    base = pl.program_id(0) * tm * TOP_K

    def issue(t, carry):
        for k in range(TOP_K):
            _row_copy(ys_ref, pos_ref[base + t * TOP_K + k], buf_ref.at[k], t, sem).start()
        return carry

    lax.fori_loop(0, tm, issue, 0)

    def wait(t, carry):
        for k in range(TOP_K):
            _row_copy(ys_ref, 0, buf_ref.at[k], t, sem).wait()
        return carry

    lax.fori_loop(0, tm, wait, 0)
    wt = wt_ref[...]
    acc = buf_ref[0] * wt[:, 0:1]
    for k in range(1, TOP_K):
        acc = acc + buf_ref[k] * wt[:, k:k + 1]
    o_ref[...] = x_ref[...] + gate_ref[...] * acc


def _combine(pos, ys, wt, x, gate, tm, rows_per_group):
    m, d = x.shape
    r = gate.shape[1]
    tiles = rows_per_group // tm
    return pl.pallas_call(
        _combine_kernel,
        out_shape=jax.ShapeDtypeStruct((m, d), F32),
        grid_spec=pltpu.PrefetchScalarGridSpec(
            num_scalar_prefetch=1,
            grid=(m // tm,),
            in_specs=[
                pl.BlockSpec(memory_space=pl.ANY),
                pl.BlockSpec((tm, LANES), lambda i, pos: (i, 0)),
                pl.BlockSpec((tm, d), lambda i, pos: (i, 0)),
                pl.BlockSpec((None, r, d), lambda i, pos: (i // tiles, 0, 0)),
            ],
            out_specs=pl.BlockSpec((tm, d), lambda i, pos: (i, 0)),
            scratch_shapes=[pltpu.VMEM((TOP_K, tm, d), F32), pltpu.SemaphoreType.DMA],
        ),
        compiler_params=_params(("arbitrary",)),
        name="moe_combine",
    )(pos, ys, wt, x, gate)


TM = 512
TN = 1024


def _pad_lanes(a):
    return jnp.pad(a, ((0, 0), (0, LANES - a.shape[1])))


def _layer_weights(l, d, w_in, fox_f_bias, gdn_a_log, gdn_dt_bias, w_router, b_router):
    w = w_in[l]
    o_f = 3 * MIX_W
    o_gqkv = o_f + N_HEADS
    o_ga = o_gqkv + 3 * MIX_W
    o_gb = o_ga + N_HEADS
    o_gz = o_gb + N_HEADS
    o_ret = o_gz + MIX_W
    o_gate = o_ret + 4 * MIX_W
    small = _pad_lanes(jnp.concatenate([w[:, o_f:o_gqkv], w[:, o_ga:o_gb], w[:, o_gb:o_gz]], axis=1))
    ws_hi, ws_lo = _split2(small)
    prm = jnp.zeros((SUBLANES, LANES), F32)
    prm = prm.at[0, 0:8].set(fox_f_bias[l]).at[0, 8:16].set(gdn_dt_bias[l])
    prm = prm.at[1, 8:16].set(-jnp.exp(gdn_a_log[l]))
    wr_hi, wr_lo = _split2(_pad_lanes(w_router[l]))
    return dict(
        w_fox=w[:, 0:o_f].astype(BF16),
        w_gqkv=w[:, o_gqkv:o_ga].astype(BF16),
        w_gz=w[:, o_gz:o_ret].astype(BF16),
        w_ret=w[:, o_ret:o_gate].astype(BF16),
        w_gate=w[:, o_gate:o_gate + N_BRANCH * d].astype(BF16),
        ws_hi=ws_hi, ws_lo=ws_lo, prm=prm, wr_hi=wr_hi, wr_lo=wr_lo,
        b_router=_pad_lanes(b_router[l].reshape(1, N_EXPERTS)),
    )


def _rope_tables(pos):
    half = HEAD_DIM // 2
    inv = ROPE_BASE ** (-jnp.arange(half, dtype=F32) / half)
    ang = pos.astype(F32)[:, None] * inv[None, :]
    cos, sin = jnp.cos(ang), jnp.sin(ang)
    return jnp.concatenate([cos, cos], axis=1), jnp.concatenate([-sin, sin], axis=1)


def _project(h, lw, tm):
    fox = _mm(h, lw['w_fox'], tm, TN)
    gqkv = _mm(h, lw['w_gqkv'], tm, TN)
    gz = _mm(h, lw['w_gz'], tm, TN)
    ret = _mm(h, lw['w_ret'], tm, TN)
    gate = _mm(h, lw['w_gate'], tm, TN)
    gates = _gates(h, lw['ws_hi'], lw['ws_lo'], lw['prm'], tm)
    return fox, gqkv, gz, ret, gate, gates


def _prompt_mixer(x, mod, lw, p, batch, seq):
    m, d = x.shape
    sh1, sc1, g1, sh2, sc2, g2 = mod
    h = _norm_mod(x, p['norm1'], sc1, sh1, TM, seq)
    fox, gqkv, gz, ret, gate, gates = _project(h, lw, TM)
    cum = _cumsum_rows(gates, 256, seq)
    fq16, fk, fk16, fv16 = _fox_prep(fox, p['fox_qn'], p['fox_kn'], TM)
    cum_t = cum[:, :N_HEADS].reshape(batch, seq, N_HEADS).transpose(0, 2, 1)
    o_fox = _flash_attention(fq16, fk16, fv16, cum_t[..., None], cum_t[:, :, None, :], batch, seq, 512)
    n_chunks = seq // CHUNK
    gates_t = gates.reshape(m // CHUNK, CHUNK, LANES).transpose(0, 2, 1)
    zero_state = jnp.zeros((batch, N_HEADS, HEAD_DIM, HEAD_DIM), F32)
    o_gdn, s_gdn = _gdn(gqkv, p['gdn_conv'], jnp.zeros((batch, SUBLANES, 3 * MIX_W), F32), gates, gates_t, gz,
                        p['gdn_nw'], zero_state, batch, n_chunks, CHUNK, BF16)
    cos, sin = _rope_tables(jnp.arange(seq, dtype=jnp.int32))
    o_ret, s_ret = _retention(ret, cos, sin, zero_state, batch, n_chunks, CHUNK, CHUNK, BF16)
    merged = _merge(o_fox, o_gdn, o_ret, p['w_branch'], p['layer'], gate, TM, 512)
    x1 = _mm_residual(merged, p['w_out'], p['layer'], x, g1, TM, TN, seq)
    h2 = _norm_mod(x1, p['norm2'], sc2, sh2, TM, seq, F32)
    state = (fk.reshape(batch, seq, N_HEADS, HEAD_DIM),
             fox[:, 2 * MIX_W:].reshape(batch, seq, N_HEADS, HEAD_DIM),
             gates[:, :N_HEADS].reshape(batch, seq, N_HEADS),
             s_gdn,
             gqkv.reshape(batch, seq, 3 * MIX_W)[:, seq - (CONV_W - 1):],
             s_ret)
    return x1, h2, state


def _sample_mixer(x, mod, lw, p, past, layer, batch, seq, past_len):
    m, d = x.shape
    sh1, sc1, g1, sh2, sc2, g2 = mod
    n_pages = past_len // PAGE_SIZE
    h = _norm_mod(x, p['norm1'], sc1, sh1, m, m)
    fox, gqkv, gz, ret, gate, gates = _project(h, lw, m)
    cum = _cumsum_rows(gates, m, seq)
    fq16, fk, fk16, fv16 = _fox_prep(fox, p['fox_qn'], p['fox_kn'], m)

    pad_q = SUBLANES - seq
    def head_major(a, pad_to):
        a = a.reshape(batch, seq, N_HEADS, HEAD_DIM).transpose(0, 2, 1, 3)
        return jnp.pad(a, ((0, 0), (0, 0), (0, pad_to - seq), (0, 0)))

    q64 = head_major(fq16, SUBLANES).reshape(batch, N_HEADS * SUBLANES, HEAD_DIM).astype(F32)
    cum_t = cum[:, :N_HEADS].reshape(batch, seq, N_HEADS).transpose(0, 2, 1)
    cq = jnp.pad(cum_t, ((0, 0), (0, 0), (0, pad_q))).reshape(batch, N_HEADS * SUBLANES, 1)
    kn = head_major(fk16, LANES).reshape(batch, N_HEADS * LANES, HEAD_DIM)
    vn = head_major(fv16, LANES).reshape(batch, N_HEADS * LANES, HEAD_DIM)
    ckn = jnp.pad(cum_t, ((0, 0), (0, 0), (0, LANES - seq)))
    bias = _suffix_sums(past['pt'], past['logf_t'], layer, batch, n_pages)
    o = _decode_attention(past['pt'], q64, past['k'], past['v'], bias, cq, kn, vn, ckn, layer, batch, n_pages)
    o_fox = (o.reshape(batch, N_HEADS, SUBLANES, HEAD_DIM)[:, :, :seq].transpose(0, 2, 1, 3)
             .reshape(m, MIX_W).astype(BF16))

    def pad_rows(a):
        return jnp.pad(a.reshape(batch, seq, -1), ((0, 0), (0, pad_q), (0, 0))).reshape(batch * SUBLANES, -1)

    def unpad_rows(a):
        return a.reshape(batch, SUBLANES, -1)[:, :seq].reshape(m, -1)

    gates_p = pad_rows(gates)
    gates_t = gates_p.reshape(batch, SUBLANES, LANES).transpose(0, 2, 1)
    conv0 = jnp.pad(past['conv'], ((0, 0), (SUBLANES - (CONV_W - 1), 0), (0, 0)))
    o_gdn, s_gdn = _gdn(pad_rows(gqkv), p['gdn_conv'], conv0, gates_p, gates_t, pad_rows(gz), p['gdn_nw'],
                        past['gdn'], batch, 1, SUBLANES, F32)
    cos, sin = _rope_tables(past_len + jnp.arange(seq, dtype=jnp.int32))
    cos = jnp.pad(cos, ((0, pad_q), (0, 0)))
    sin = jnp.pad(sin, ((0, pad_q), (0, 0)))
    o_ret, s_ret = _retention(pad_rows(ret), cos, sin, past['ret'], batch, 1, SUBLANES, seq, F32)
    o_gdn = unpad_rows(o_gdn).astype(BF16)
    o_ret = unpad_rows(o_ret).astype(BF16)

    merged = _merge(o_fox, o_gdn, o_ret, p['w_branch'], p['layer'], gate, m, 512)
    x1 = _mm_residual(merged, p['w_out'], p['layer'], x, g1, m, TN, m)
    h2 = _norm_mod(x1, p['norm2'], sc2, sh2, m, m, F32)
    state = (fk.reshape(batch, seq, N_HEADS, HEAD_DIM),
             fox[:, 2 * MIX_W:].reshape(batch, seq, N_HEADS, HEAD_DIM),
             gates[:, :N_HEADS].reshape(batch, seq, N_HEADS),
             s_gdn,
             gqkv.reshape(batch, seq, 3 * MIX_W)[:, seq - (CONV_W - 1):],
             s_ret)
    return x1, h2, state


def _moe(h_ext, n, lw, w_e1, b_e1, w_e2, b_e2, layer):
    idx, wt, cnt = _router(h_ext, n, lw['wr_hi'], lw['wr_lo'], lw['b_router'], LANES)
    flat_e = idx[:, :TOP_K].reshape(-1)
    rank = idx[:, TOP_K:2 * TOP_K].reshape(-1)
    counts = cnt[0, :N_EXPERTS].astype(jnp.int32)
    n_assign = n * TOP_K
    padded = (counts + MOE_BLK - 1) // MOE_BLK * MOE_BLK
    pad_end = jnp.cumsum(padded)
    dest = ((pad_end - padded)[flat_e] + rank).astype(jnp.int32)
    n_blk = -(-n_assign // MOE_BLK) + N_EXPERTS
    slot_tok = jnp.full((n_blk * MOE_BLK,), n, jnp.int32).at[dest].set(
        jnp.arange(n_assign, dtype=jnp.int32) // TOP_K)
    blk_ids = jnp.arange(n_blk, dtype=jnp.int32)
    blk_e = jnp.sum((blk_ids[:, None] * MOE_BLK >= pad_end[None, :]).astype(jnp.int32), axis=1)
    n_valid = (pad_end[-1] // MOE_BLK).astype(jnp.int32)
    blk_e = jnp.where(blk_ids < n_valid, jnp.minimum(blk_e, N_EXPERTS - 1), 0)
    blk_e = jnp.where(blk_ids < n_valid, blk_e, jnp.max(blk_e)).astype(jnp.int32)
    n_valid = n_valid.reshape(1)
    xs = _gather_rows(slot_tok, n_valid, h_ext, n_blk, MOE_BLK)
    tn = min(MOE_TN, w_e2.shape[2], w_e2.shape[3])
    act = _gmm1(blk_e, n_valid, xs, w_e1, b_e1, layer, n_blk, MOE_BLK, tn)
    ys = _gmm2(blk_e, n_valid, act, w_e2, b_e2, layer, n_blk, MOE_BLK, tn)
    return ys, dest, wt


def kernel(x_prompt, x_sample, cache_fox_k, cache_fox_v, cache_fox_logf, state_gdn, state_gdn_conv, state_ret,
           page_table, c_prompt, c_sample, norm1_w, norm2_w, w_ada, b_ada, w_in, fox_f_bias, fox_qn_w, fox_kn_w,
           gdn_conv_w, gdn_a_log, gdn_dt_bias, gdn_norm_w, w_branch, w_out, w_router, b_router,
           w_e1, b_e1, w_e2, b_e2):
    batch, seq, d = x_prompt.shape
    dec_batch, dec_seq, _ = x_sample.shape
    depth = w_in.shape[0]
    n_pool = cache_fox_k.shape[1]
    n_pages = page_table.shape[1]
    past_len = n_pages * PAGE_SIZE
    mp = batch * seq
    ms = dec_batch * dec_seq

    n_cond = batch + dec_batch
    c_all = jnp.concatenate([c_prompt, c_sample, jnp.zeros((-n_cond % SUBLANES, d), F32)], axis=0)
    mod_all = _modulation(c_all, w_ada, b_ada)

    cache_k = cache_fox_k.reshape(depth, n_pool, PAGE_SIZE * N_HEADS, HEAD_DIM)
    cache_v = cache_fox_v.reshape(depth, n_pool, PAGE_SIZE * N_HEADS, HEAD_DIM)
    logf_t = cache_fox_logf.transpose(0, 1, 3, 2)
    pt_flat = page_table.reshape(-1).astype(jnp.int32)

    xp = x_prompt.reshape(mp, d)
    xs = x_sample.reshape(ms, d)
    st_p, st_s = [], []
    for l in range(depth):
        lw = _layer_weights(l, d, w_in, fox_f_bias, gdn_a_log, gdn_dt_bias, w_router, b_router)
        p = dict(norm1=norm1_w[l], norm2=norm2_w[l], fox_qn=fox_qn_w[l], fox_kn=fox_kn_w[l],
                 gdn_conv=gdn_conv_w[l], gdn_nw=gdn_norm_w[l], w_branch=w_branch, w_out=w_out, layer=l)
        mod_p = [t[:, None, :] for t in jnp.split(mod_all[l, :batch], 6, axis=-1)]
        mod_s = [t[None] for t in jnp.split(jnp.repeat(mod_all[l, batch:n_cond], dec_seq, axis=0), 6, axis=-1)]
        past = dict(pt=pt_flat, k=cache_k, v=cache_v, logf_t=logf_t, gdn=state_gdn[l],
                    conv=state_gdn_conv[l], ret=state_ret[l])
        x1p, h2p, sp = _prompt_mixer(xp, mod_p, lw, p, batch, seq)
        x1s, h2s, ss = _sample_mixer(xs, mod_s, lw, p, past, l, dec_batch, dec_seq, past_len)
        h_ext = jnp.concatenate([h2p, h2s, jnp.zeros((SUBLANES, d), F32)], axis=0)
        ys, dest, wt = _moe(h_ext, mp + ms, lw, w_e1, b_e1, w_e2, b_e2, l)
        xp = _combine(dest[:mp * TOP_K], ys, wt[:mp], x1p, mod_p[5], LANES, seq)
        xs = _combine(dest[mp * TOP_K:], ys, wt[mp:], x1s, mod_s[5], LANES, ms)
        st_p.append(sp)
        st_s.append(ss)

    def stk(sts, i):
        return jnp.stack([s[i] for s in sts])

    return (xp.reshape(batch, seq, d), xs.reshape(dec_batch, dec_seq, d),
            stk(st_p, 0), stk(st_p, 1), stk(st_p, 2), stk(st_p, 3), stk(st_p, 4), stk(st_p, 5),
            stk(st_s, 0), stk(st_s, 1), stk(st_s, 2), stk(st_s, 3), stk(st_s, 4), stk(st_s, 5))
```

```python
import functools
import math

import jax
import jax.numpy as jnp
from jax import lax
from jax.experimental import pallas as pl
from jax.experimental.pallas import tpu as pltpu

F32 = jnp.float32
BF16 = jnp.bfloat16

HEAD_DIM = 128
N_HEADS = 8
MIX_W = N_HEADS * HEAD_DIM
N_BRANCH = 3
CONV_W = 4
CHUNK = 64
PAGE_SIZE = 128
ROPE_BASE = 10000.0
N_EXPERTS = 32
TOP_K = 4
SWIGLU_LIMIT = 7.0
SWIGLU_ALPHA = 1.702
EPS = 1e-6

LANES = 128
SUBLANES = 8
VMEM_LIMIT = 48 * 1024 * 1024
MOE_VMEM_LIMIT = 56 * 1024 * 1024
MOE_BLK = 256
MOE_TN = 1024
NEG_BIG = -1e30


def _params(sem, vmem=VMEM_LIMIT):
    return pltpu.CompilerParams(dimension_semantics=sem, vmem_limit_bytes=vmem)


def _dot(a, b):
    return jnp.dot(a, b, preferred_element_type=F32)


def _dot_nt(a, b):
    return lax.dot_general(a, b, (((1,), (1,)), ((), ())), preferred_element_type=F32)


def _dot_tn(a, b):
    return lax.dot_general(a, b, (((0,), (0,)), ((), ())), preferred_element_type=F32)


def _split2(a):
    hi = a.astype(BF16)
    lo = (a - hi.astype(F32)).astype(BF16)
    return hi, lo


def _split3(a):
    hi = a.astype(BF16)
    r = a - hi.astype(F32)
    mid = r.astype(BF16)
    lo = (r - mid.astype(F32)).astype(BF16)
    return hi, mid, lo


def _dot_exact_lhs(mask_bf16, x):
    hi, mid, lo = _split3(x)
    return _dot(mask_bf16, hi) + (_dot(mask_bf16, mid) + _dot(mask_bf16, lo))


def _dot_exact_rhs(x, mask_bf16):
    hi, mid, lo = _split3(x)
    return _dot(hi, mask_bf16) + (_dot(mid, mask_bf16) + _dot(lo, mask_bf16))


def _sigmoid(x):
    return 1.0 / (1.0 + jnp.exp(-x))


def _silu(x):
    return x * _sigmoid(x)


def _softplus(x):
    return jnp.maximum(x, 0.0) + jnp.log1p(jnp.exp(-jnp.abs(x)))


def _iota2(shape, dim):
    return lax.broadcasted_iota(jnp.int32, shape, dim)


def _modulation_kernel(c_ref, w_ref, b_ref, o_ref):
    c = _silu(c_ref[...])
    c_hi, c_lo = _split2(c)
    w16 = w_ref[...].astype(BF16)
    o_ref[...] = _dot(c_hi, w16) + _dot(c_lo, w16) + b_ref[...]


def _modulation(c_all, w_ada, b_ada, tn=1024):
    depth, d, n = w_ada.shape
    rows = c_all.shape[0]
    return pl.pallas_call(
        _modulation_kernel,
        out_shape=jax.ShapeDtypeStruct((depth, rows, n), F32),
        grid=(depth, n // tn),
        in_specs=[
            pl.BlockSpec((rows, d), lambda l, j: (0, 0)),
            pl.BlockSpec((None, d, tn), lambda l, j: (l, 0, j)),
            pl.BlockSpec((None, 1, tn), lambda l, j: (l, 0, j)),
        ],
        out_specs=pl.BlockSpec((None, rows, tn), lambda l, j: (l, 0, j)),
        compiler_params=_params(("arbitrary", "arbitrary")),
        name="adaln_modulation",
    )(c_all, w_ada, b_ada.reshape(depth, 1, n))


def _rms(x):
    return x * lax.rsqrt(jnp.mean(x * x, axis=-1, keepdims=True) + EPS)


def _norm_mod_kernel(x_ref, w_ref, sc_ref, sh_ref, o_ref):
    y = _rms(x_ref[...]) * w_ref[...]
    o_ref[...] = (y * (1.0 + sc_ref[...]) + sh_ref[...]).astype(o_ref.dtype)


def _mod_spec(mod, tm, rows_per_group):
    r, d = mod.shape[1], mod.shape[2]
    tiles = rows_per_group // tm
    return pl.BlockSpec((None, r, d), lambda i: (i // tiles, 0, 0))


def _norm_mod(x, w, sc, sh, tm, rows_per_group, out_dtype=BF16):
    m, d = x.shape
    return pl.pallas_call(
        _norm_mod_kernel,
        out_shape=jax.ShapeDtypeStruct((m, d), out_dtype),
        grid=(m // tm,),
        in_specs=[
            pl.BlockSpec((tm, d), lambda i: (i, 0)),
            pl.BlockSpec((1, d), lambda i: (0, 0)),
            _mod_spec(sc, tm, rows_per_group),
            _mod_spec(sh, tm, rows_per_group),
        ],
        out_specs=pl.BlockSpec((tm, d), lambda i: (i, 0)),
        compiler_params=_params(("arbitrary",)),
        name="norm_modulate",
    )(x, w.reshape(1, d), sc, sh)


def _mm_kernel(a_ref, w_ref, o_ref, w16_ref):
    @pl.when(pl.program_id(1) == 0)
    def _():
        w16_ref[...] = w_ref[...].astype(BF16)

    o_ref[...] = _dot(a_ref[...], w16_ref[...]).astype(o_ref.dtype)


def _mm_residual_kernel(a_ref, w_ref, res_ref, gate_ref, o_ref, w16_ref):
    @pl.when(pl.program_id(1) == 0)
    def _():
        w16_ref[...] = w_ref[...].astype(BF16)

    o_ref[...] = res_ref[...] + gate_ref[...] * _dot(a_ref[...], w16_ref[...])


def _mm(a, w, tm, tn, out_dtype=F32):
    m, k = a.shape
    n = w.shape[1]
    return pl.pallas_call(
        _mm_kernel,
        out_shape=jax.ShapeDtypeStruct((m, n), out_dtype),
        grid=(n // tn, m // tm),
        in_specs=[
            pl.BlockSpec((tm, k), lambda j, i: (i, 0)),
            pl.BlockSpec((k, tn), lambda j, i: (0, j)),
        ],
        out_specs=pl.BlockSpec((tm, tn), lambda j, i: (i, j)),
        scratch_shapes=[pltpu.VMEM((k, tn), BF16)],
        compiler_params=_params(("arbitrary", "arbitrary")),
        name="matmul",
    )(a, w)


def _mm_residual(a, w, layer, res, gate, tm, tn, rows_per_group):
    m, k = a.shape
    n = w.shape[2]
    r = gate.shape[1]
    tiles = rows_per_group // tm
    return pl.pallas_call(
        _mm_residual_kernel,
        out_shape=jax.ShapeDtypeStruct((m, n), F32),
        grid=(n // tn, m // tm),
        in_specs=[
            pl.BlockSpec((tm, k), lambda j, i: (i, 0)),
            pl.BlockSpec((None, k, tn), lambda j, i: (layer, 0, j)),
            pl.BlockSpec((tm, tn), lambda j, i: (i, j)),
            pl.BlockSpec((None, r, tn), lambda j, i: (i // tiles, 0, j)),
        ],
        out_specs=pl.BlockSpec((tm, tn), lambda j, i: (i, j)),
        scratch_shapes=[pltpu.VMEM((k, tn), BF16)],
        compiler_params=_params(("arbitrary", "arbitrary")),
        name="matmul_residual",
    )(a, w, res, gate)


def _gates_kernel(h_ref, whi_ref, wlo_ref, prm_ref, o_ref):
    h = h_ref[...]
    z = _dot(h, whi_ref[...]) + _dot(h, wlo_ref[...]) + prm_ref[0:1, :]
    lane = _iota2(z.shape, 1)
    tail = jnp.log1p(jnp.exp(-jnp.abs(z)))
    lf = jnp.minimum(z, 0.0) - tail
    g = prm_ref[1:2, :] * (jnp.maximum(z, 0.0) + tail)
    beta = _sigmoid(z)
    o_ref[...] = jnp.where(lane < 8, lf, jnp.where(lane < 16, g, beta))


def _gates(h, w_hi, w_lo, prm, tm):
    m, d = h.shape
    return pl.pallas_call(
        _gates_kernel,
        out_shape=jax.ShapeDtypeStruct((m, LANES), F32),
        grid=(m // tm,),
        in_specs=[
            pl.BlockSpec((tm, d), lambda i: (i, 0)),
            pl.BlockSpec((d, LANES), lambda i: (0, 0)),
            pl.BlockSpec((d, LANES), lambda i: (0, 0)),
            pl.BlockSpec((SUBLANES, LANES), lambda i: (0, 0)),
        ],
        out_specs=pl.BlockSpec((tm, LANES), lambda i: (i, 0)),
        compiler_params=_params(("arbitrary",)),
        name="gate_projection",
    )(h, w_hi, w_lo, prm)


def _cumsum_kernel(x_ref, o_ref, carry_ref, *, group):
    tm = x_ref.shape[0]
    r = _iota2((tm, tm), 0)
    c = _iota2((tm, tm), 1)
    if group >= tm:
        tri = jnp.where(c <= r, 1.0, 0.0).astype(BF16)
    else:
        tri = jnp.where(c <= r, jnp.where(r // group == c // group, 1.0, 0.0), 0.0).astype(BF16)
    cs = _dot_exact_lhs(tri, x_ref[...])
    if group > tm:
        @pl.when(pl.program_id(0) % (group // tm) == 0)
        def _():
            carry_ref[...] = jnp.zeros_like(carry_ref)

        cs = cs + carry_ref[...]
        carry_ref[...] = cs[tm - 1:tm, :]
    o_ref[...] = cs


def _cumsum_rows(x, tm, group):
    m, n = x.shape
    return pl.pallas_call(
        functools.partial(_cumsum_kernel, group=group),
        out_shape=jax.ShapeDtypeStruct((m, n), F32),
        grid=(m // tm,),
        in_specs=[pl.BlockSpec((tm, n), lambda i: (i, 0))],
        out_specs=pl.BlockSpec((tm, n), lambda i: (i, 0)),
        scratch_shapes=[pltpu.VMEM((1, n), F32)],
        compiler_params=_params(("arbitrary",)),
        name="cumsum_rows",
    )(x)


def _fox_prep_kernel(q_ref, k_ref, v_ref, qw_ref, kw_ref, fq16_ref, fk_ref, fk16_ref, fv16_ref):
    for h in range(N_HEADS):
        sl = slice(h * HEAD_DIM, (h + 1) * HEAD_DIM)
        fq16_ref[:, sl] = (_rms(q_ref[:, sl]) * qw_ref[...]).astype(BF16)
        kn = _rms(k_ref[:, sl]) * kw_ref[...]
        fk_ref[:, sl] = kn
        fk16_ref[:, sl] = kn.astype(BF16)
    fv16_ref[...] = v_ref[...].astype(BF16)


def _fox_prep(qkv, qw, kw, tm):
    m = qkv.shape[0]
    col = lambda c: pl.BlockSpec((tm, MIX_W), lambda i: (i, c))
    wspec = pl.BlockSpec((1, HEAD_DIM), lambda i: (0, 0))
    return pl.pallas_call(
        _fox_prep_kernel,
        out_shape=(
            jax.ShapeDtypeStruct((m, MIX_W), BF16),
            jax.ShapeDtypeStruct((m, MIX_W), F32),
            jax.ShapeDtypeStruct((m, MIX_W), BF16),
            jax.ShapeDtypeStruct((m, MIX_W), BF16),
        ),
        grid=(m // tm,),
        in_specs=[col(0), col(1), col(2), wspec, wspec],
        out_specs=(col(0), col(0), col(0), col(0)),
        compiler_params=_params(("arbitrary",)),
        name="fox_qk_norm",
    )(qkv, qkv, qkv, qw.reshape(1, HEAD_DIM), kw.reshape(1, HEAD_DIM))


def _online_softmax_update(s, v16, m_ref, l_ref, acc_ref):
    m_prev = m_ref[...]
    m_new = jnp.maximum(m_prev, jnp.max(s, axis=1, keepdims=True))
    alpha = jnp.exp(m_prev - m_new)
    p = jnp.exp(s - m_new)
    l_ref[...] = alpha * l_ref[...] + jnp.sum(p, axis=1, keepdims=True)
    acc_ref[...] = alpha * acc_ref[...] + _dot(p.astype(BF16), v16)
    m_ref[...] = m_new


FLASH_HEADS = 2


def _flash_kernel(qi_ref, kj_ref, q_ref, k_ref, v_ref, cq_ref, ck_ref, o_ref, m_ref, l_ref, acc_ref, *, scale):
    i = qi_ref[pl.program_id(2)]
    j = kj_ref[pl.program_id(2)]

    @pl.when(j == 0)
    def _():
        m_ref[...] = jnp.full_like(m_ref, NEG_BIG)
        l_ref[...] = jnp.zeros_like(l_ref)
        acc_ref[...] = jnp.zeros_like(acc_ref)

    heads = range(m_ref.shape[0])
    cols = [slice(g * HEAD_DIM, (g + 1) * HEAD_DIM) for g in heads]

    def update(diagonal):
        ss = [_dot_nt(q_ref[:, cols[g]], k_ref[:, cols[g]]) * scale + (cq_ref[g] - ck_ref[g]) for g in heads]
        if diagonal:
            keep = _iota2(ss[0].shape, 0) >= _iota2(ss[0].shape, 1)
            ss = [jnp.where(keep, s, NEG_BIG) for s in ss]
        m_prev = [m_ref[g] for g in heads]
        m_new = [jnp.maximum(m_prev[g], jnp.max(ss[g], axis=1, keepdims=True)) for g in heads]
        alpha = [jnp.exp(m_prev[g] - m_new[g]) for g in heads]
        ps = [jnp.exp(ss[g] - m_new[g]) for g in heads]
        ls = [alpha[g] * l_ref[g] + jnp.sum(ps[g], axis=1, keepdims=True) for g in heads]
        accs = [alpha[g] * acc_ref[g] + _dot(ps[g].astype(BF16), v_ref[:, cols[g]]) for g in heads]
        for g in heads:
            m_ref[g] = m_new[g]
            l_ref[g] = ls[g]
            acc_ref[g] = accs[g]

    @pl.when(j < i)
    def _():
        update(False)

    @pl.when(j == i)
    def _():
        update(True)
        for g in heads:
            o_ref[:, cols[g]] = (acc_ref[g] / l_ref[g]).astype(o_ref.dtype)


def _flash_attention(fq16, fk16, fv16, cum_col, cum_row, batch, seq, t):
    n = seq // t
    pairs = [(i, j) for i in range(n) for j in range(i + 1)]
    qi = jnp.asarray([i for i, _ in pairs], jnp.int32)
    kj = jnp.asarray([j for _, j in pairs], jnp.int32)
    g = FLASH_HEADS
    qspec = pl.BlockSpec((t, g * HEAD_DIM), lambda b, h, s, qi, kj: (b * n + qi[s], h))
    kspec = pl.BlockSpec((t, g * HEAD_DIM), lambda b, h, s, qi, kj: (b * n + kj[s], h))
    return pl.pallas_call(
        functools.partial(_flash_kernel, scale=HEAD_DIM ** -0.5),
        out_shape=jax.ShapeDtypeStruct((batch * seq, MIX_W), BF16),
        grid_spec=pltpu.PrefetchScalarGridSpec(
            num_scalar_prefetch=2,
            grid=(batch, N_HEADS // g, len(pairs)),
            in_specs=[
                qspec, kspec, kspec,
                pl.BlockSpec((None, g, t, 1), lambda b, h, s, qi, kj: (b, h, qi[s], 0)),
                pl.BlockSpec((None, g, 1, t), lambda b, h, s, qi, kj: (b, h, 0, kj[s])),
            ],
            out_specs=qspec,
            scratch_shapes=[pltpu.VMEM((g, t, 1), F32), pltpu.VMEM((g, t, 1), F32),
                            pltpu.VMEM((g, t, HEAD_DIM), F32)],
        ),
        compiler_params=_params(("arbitrary", "arbitrary", "arbitrary")),
        name="fox_flash_attention",
    )(qi, kj, fq16, fk16, fv16, cum_col, cum_row)


def _suffix_kernel(pt_ref, x_ref, o_ref, xs_ref, *, n_pages):
    b = pl.program_id(0)
    for p in range(n_pages):
        xs_ref[p * N_HEADS:(p + 1) * N_HEADS, :] = x_ref[pt_ref[b * n_pages + p]]
    x = xs_ref[...]
    rows, t = x.shape
    later = jnp.where(_iota2((t, t), 0) > _iota2((t, t), 1), 1.0, 0.0).astype(BF16)
    r = _iota2((rows, rows), 0)
    col = _iota2((rows, rows), 1)
    later_pages = jnp.where(col // N_HEADS > r // N_HEADS,
                            jnp.where(col % N_HEADS == r % N_HEADS, 1.0, 0.0), 0.0).astype(BF16)
    totals = jnp.broadcast_to(jnp.sum(x, axis=1, keepdims=True), x.shape)
    o_ref[...] = _dot_exact_rhs(x, later) + _dot_exact_lhs(later_pages, totals)


def _suffix_sums(page_table_flat, logf_t, layer, n_batch, n_pages):
    n_pool = logf_t.shape[1]
    rows = n_pages * N_HEADS
    return pl.pallas_call(
        functools.partial(_suffix_kernel, n_pages=n_pages),
        out_shape=jax.ShapeDtypeStruct((n_batch, rows, PAGE_SIZE), F32),
        grid_spec=pltpu.PrefetchScalarGridSpec(
            num_scalar_prefetch=1,
            grid=(n_batch,),
            in_specs=[pl.BlockSpec((None, n_pool, N_HEADS, PAGE_SIZE), lambda b, pt: (layer, 0, 0, 0))],
            out_specs=pl.BlockSpec((None, rows, PAGE_SIZE), lambda b, pt: (b, 0, 0)),
            scratch_shapes=[pltpu.VMEM((rows, PAGE_SIZE), F32)],
        ),
        compiler_params=_params(("arbitrary",)),
        name="fox_past_suffix",
    )(page_table_flat, logf_t)


DECODE_PAGES = 8


def _decode_kernel(pt_ref, q_ref, *refs, scale, npg):
    k_refs, v_refs = refs[:npg], refs[npg:2 * npg]
    bias_ref, cq_ref, kn_ref, vn_ref, ckn_ref, o_ref, m_ref, l_ref, acc_ref = refs[2 * npg:]
    p = pl.program_id(1)

    @pl.when(p == 0)
    def _():
        m_ref[...] = jnp.full_like(m_ref, NEG_BIG)
        l_ref[...] = jnp.zeros_like(l_ref)
        acc_ref[...] = jnp.zeros_like(acc_ref)

    def head_rows(h):
        return slice(h * SUBLANES, (h + 1) * SUBLANES)

    def head_of_page(ref, h):
        return ref[pl.ds(h, PAGE_SIZE, stride=N_HEADS), :].astype(BF16)

    heads = range(N_HEADS)
    q16s = [q_ref[head_rows(h), :].astype(BF16) for h in heads]
    cqs = [cq_ref[head_rows(h), :] for h in heads]
    bias = bias_ref[...]
    ss = [jnp.concatenate(
        [_dot_nt(q16s[h], head_of_page(k_refs[i], h)) * scale
         + (cqs[h] + bias[i * N_HEADS + h:i * N_HEADS + h + 1, :]) for i in range(npg)], axis=1) for h in heads]
    m_prev = m_ref[...]
    m_new = jnp.maximum(m_prev, jnp.concatenate([jnp.max(s, axis=1, keepdims=True) for s in ss], axis=0))
    alpha = jnp.exp(m_prev - m_new)
    ps = [jnp.exp(ss[h] - m_new[head_rows(h), :]) for h in heads]
    l_ref[...] = alpha * l_ref[...] + jnp.concatenate([jnp.sum(p, axis=1, keepdims=True) for p in ps], axis=0)
    pv = jnp.concatenate(
        [_dot(ps[h].astype(BF16), jnp.concatenate([head_of_page(v_refs[i], h) for i in range(npg)], axis=0))
         for h in heads], axis=0)
    acc_ref[...] = alpha * acc_ref[...] + pv
    m_ref[...] = m_new

    @pl.when(p == pl.num_programs(1) - 1)
    def _():
        for h in range(N_HEADS):
            rows = head_rows(h)
            new = slice(h * LANES, (h + 1) * LANES)
            s2 = _dot_nt(q16s[h], kn_ref[new, :]) * scale + (cq_ref[rows, :] - ckn_ref[h:h + 1, :])
            s2 = jnp.where(_iota2(s2.shape, 1) <= _iota2(s2.shape, 0), s2, NEG_BIG)
            _online_softmax_update(s2, vn_ref[new, :], m_ref.at[rows], l_ref.at[rows], acc_ref.at[rows])
        o_ref[...] = acc_ref[...] / l_ref[...]


def _decode_attention(page_table_flat, q64, cache_k, cache_v, bias, cq, kn, vn, ckn, layer, n_batch, n_pages):
    rows = N_HEADS * SUBLANES
    flat = PAGE_SIZE * N_HEADS
    npg = DECODE_PAGES
    page = lambda i: pl.BlockSpec((None, None, flat, HEAD_DIM),
                                  lambda b, p, pt: (layer, pt[b * n_pages + p * npg + i], 0, 0))
    per_b = lambda shape: pl.BlockSpec((None,) + shape, lambda b, p, pt: (b, 0, 0))
    return pl.pallas_call(
        functools.partial(_decode_kernel, scale=HEAD_DIM ** -0.5, npg=npg),
        out_shape=jax.ShapeDtypeStruct((n_batch, rows, HEAD_DIM), F32),
        grid_spec=pltpu.PrefetchScalarGridSpec(
            num_scalar_prefetch=1,
            grid=(n_batch, n_pages // npg),
            in_specs=[per_b((rows, HEAD_DIM))] + [page(i) for i in range(npg)] + [page(i) for i in range(npg)] + [
                pl.BlockSpec((None, npg * N_HEADS, PAGE_SIZE), lambda b, p, pt: (b, p, 0)),
                per_b((rows, 1)),
                per_b((N_HEADS * LANES, HEAD_DIM)), per_b((N_HEADS * LANES, HEAD_DIM)), per_b((N_HEADS, LANES)),
            ],
            out_specs=per_b((rows, HEAD_DIM)),
            scratch_shapes=[pltpu.VMEM((rows, 1), F32), pltpu.VMEM((rows, 1), F32),
                            pltpu.VMEM((rows, HEAD_DIM), F32)],
        ),
        compiler_params=_params(("arbitrary", "arbitrary")),
        name="fox_paged_decode",
    )(page_table_flat, q64, *([cache_k] * npg), *([cache_v] * npg), bias, cq, kn, vn, ckn)


def _unit_lower_inverses(ms, c):
    r = _iota2((c, c), 0)
    col = _iota2((c, c), 1)
    eye = jnp.where(r == col, 1.0, 0.0)
    pair = r // 2 == col // 2
    invs = [eye - jnp.where(pair, m, 0.0) for m in ms]
    s = 2
    while s < c:
        same_block = r // (2 * s) == col // (2 * s)
        other_half = r // s != col // s
        offs = [jnp.where(same_block, jnp.where(other_half, m, 0.0), 0.0).astype(BF16) for m in ms]
        inv16s = [inv.astype(BF16) for inv in invs]
        mids = [_dot(off, inv16).astype(BF16) for off, inv16 in zip(offs, inv16s)]
        invs = [inv - _dot(inv16, mid) for inv, inv16, mid in zip(invs, inv16s, mids)]
        s *= 2
    return invs


def _gdn_kernel(qkv_ref, convw_ref, conv0_ref, gates_ref, gates_t_ref, z_ref, nw_ref, s0_ref,
                o_ref, st_ref, ext_ref, s_ref, *, c):
    n = pl.program_id(1)

    @pl.when(n == 0)
    def _():
        ext_ref[0:SUBLANES, :] = conv0_ref[...]
        s_ref[...] = s0_ref[...]

    ext_ref[SUBLANES:SUBLANES + c, :] = qkv_ref[...]

    def conv(c0):
        acc = ext_ref[SUBLANES:SUBLANES + c, c0:c0 + HEAD_DIM] * convw_ref[CONV_W - 1:CONV_W, c0:c0 + HEAD_DIM]
        for i in range(1, CONV_W):
            acc = acc + (ext_ref[SUBLANES - i:SUBLANES - i + c, c0:c0 + HEAD_DIM]
                         * convw_ref[CONV_W - 1 - i:CONV_W - i, c0:c0 + HEAD_DIM])
        return _silu(acc)

    def l2n(x):
        return x * lax.rsqrt(jnp.sum(x * x, axis=-1, keepdims=True) + EPS)

    r = _iota2((c, c), 0)
    col = _iota2((c, c), 1)
    gates = gates_ref[...]
    incl_lower = jnp.where(col <= r, 1.0, 0.0).astype(BF16)
    incl_upper = jnp.where(r <= col, 1.0, 0.0).astype(BF16)
    gc_cols = _dot_exact_lhs(incl_lower, gates)
    gc_rows = _dot_exact_rhs(gates_t_ref[...], incl_upper)

    heads = range(N_HEADS)
    qs = [l2n(conv(h * HEAD_DIM)) * (HEAD_DIM ** -0.5) for h in heads]
    ks = [l2n(conv(MIX_W + h * HEAD_DIM)) for h in heads]
    vs = [conv(2 * MIX_W + h * HEAD_DIM) for h in heads]
    ext_ref[0:SUBLANES, :] = ext_ref[c:c + SUBLANES, :]
    states = [s_ref[h] for h in heads]
    betas = [gates[:, 16 + h:17 + h] for h in heads]
    gcs = [gc_cols[:, 8 + h:9 + h] for h in heads]
    gls = [gc[c - 1:c, :] for gc in gcs]
    decays = [jnp.exp(jnp.where(col <= r, gcs[h] - gc_rows[8 + h:9 + h, :], NEG_BIG)) for h in heads]
    kbs = [ks[h] * betas[h] for h in heads]
    k16s = [k.astype(BF16) for k in ks]
    ms = [jnp.where(col < r, _dot_nt(kbs[h].astype(BF16), k16s[h]) * decays[h], 0.0) for h in heads]
    a16s = [(_dot_nt(qs[h].astype(BF16), k16s[h]) * decays[h]).astype(BF16) for h in heads]
    t16s = [t.astype(BF16) for t in _unit_lower_inverses(ms, c)]
    u0s = [_dot(t16s[h], (vs[h] * betas[h]).astype(BF16)) for h in heads]
    wk16s = [_dot(t16s[h], (kbs[h] * jnp.exp(gcs[h])).astype(BF16)).astype(BF16) for h in heads]
    s16s = [s.astype(BF16) for s in states]
    u16s = [(u0s[h] - _dot(wk16s[h], s16s[h])).astype(BF16) for h in heads]
    outs = [_dot((qs[h] * jnp.exp(gcs[h])).astype(BF16), s16s[h]) + _dot(a16s[h], u16s[h]) for h in heads]
    new_states = [states[h] * jnp.exp(gls[h]) + _dot_tn((ks[h] * jnp.exp(gls[h] - gcs[h])).astype(BF16), u16s[h])
                  for h in heads]
    for h in heads:
        s_ref[h] = new_states[h]
        zh = z_ref[:, h * HEAD_DIM:(h + 1) * HEAD_DIM]
        o_ref[:, h * HEAD_DIM:(h + 1) * HEAD_DIM] = (_rms(outs[h]) * nw_ref[...] * _silu(zh)).astype(o_ref.dtype)

    @pl.when(n == pl.num_programs(1) - 1)
    def _():
        st_ref[...] = s_ref[...]


def _gdn(qkv, conv_w, conv0, gates, gates_t, z, norm_w, s0, batch, n_chunks, c, out_dtype):
    m = qkv.shape[0]
    w3 = 3 * MIX_W
    row = lambda width: pl.BlockSpec((c, width), lambda b, n: (b * n_chunks + n, 0))
    state = pl.BlockSpec((None, N_HEADS, HEAD_DIM, HEAD_DIM), lambda b, n: (b, 0, 0, 0))
    return pl.pallas_call(
        functools.partial(_gdn_kernel, c=c),
        out_shape=(jax.ShapeDtypeStruct((m, MIX_W), out_dtype),
                   jax.ShapeDtypeStruct((batch, N_HEADS, HEAD_DIM, HEAD_DIM), F32)),
        grid=(batch, n_chunks),
        in_specs=[
            row(w3),
            pl.BlockSpec((CONV_W, w3), lambda b, n: (0, 0)),
            pl.BlockSpec((None, SUBLANES, w3), lambda b, n: (b, 0, 0)),
            row(LANES),
            pl.BlockSpec((None, LANES, c), lambda b, n: (b * n_chunks + n, 0, 0)),
            row(MIX_W),
            pl.BlockSpec((1, HEAD_DIM), lambda b, n: (0, 0)),
            state,
        ],
        out_specs=(row(MIX_W), state),
        scratch_shapes=[pltpu.VMEM((c + SUBLANES, w3), F32),
                        pltpu.VMEM((N_HEADS, HEAD_DIM, HEAD_DIM), F32)],
        compiler_params=_params(("arbitrary", "arbitrary")),
        name="gated_deltanet",
    )(qkv, conv_w, conv0, gates, gates_t, z, norm_w.reshape(1, HEAD_DIM), s0)


def _ret_kernel(q_ref, k_ref, v_ref, g_ref, cos_ref, sin_ref, s0_ref, o_ref, st_ref, s_ref, *, c, n_valid):
    n = pl.program_id(1)

    @pl.when(n == 0)
    def _():
        s_ref[...] = s0_ref[...]

    cos = cos_ref[...]
    sin = sin_ref[...]

    def rot(x):
        return x * cos + pltpu.roll(x, HEAD_DIM // 2, 1) * sin

    r = _iota2((c, c), 0)
    col = _iota2((c, c), 1)
    cnt_r = jnp.minimum(r + 1, n_valid).astype(F32)
    cnt_c = jnp.minimum(col + 1, n_valid).astype(F32)
    cnt = cnt_r[:, 0:1]
    total = float(min(c, n_valid))

    heads = range(N_HEADS)
    lgs = [math.log(1.0 - 2.0 ** (-5.0 - h)) for h in heads]
    sls = [slice(h * HEAD_DIM, (h + 1) * HEAD_DIM) for h in heads]
    qs = [rot(q_ref[:, sl]) for sl in sls]
    ks = [rot(k_ref[:, sl]) * (HEAD_DIM ** -0.5) for sl in sls]
    v16s = [v_ref[:, sl].astype(BF16) for sl in sls]
    states = [s_ref[h] for h in heads]
    a16s = [(_dot_nt(qs[h].astype(BF16), ks[h].astype(BF16))
             * jnp.exp(jnp.where(col <= r, (cnt_r - cnt_c) * lgs[h], NEG_BIG))).astype(BF16) for h in heads]
    outs = [_dot(a16s[h], v16s[h]) + _dot((qs[h] * jnp.exp(cnt * lgs[h])).astype(BF16), states[h].astype(BF16))
            for h in heads]
    new_states = [states[h] * math.exp(total * lgs[h])
                  + _dot_tn((ks[h] * jnp.exp((total - cnt) * lgs[h])).astype(BF16), v16s[h]) for h in heads]
    for h in heads:
        s_ref[h] = new_states[h]
        o_ref[:, sls[h]] = (_rms(outs[h]) * _silu(g_ref[:, sls[h]])).astype(o_ref.dtype)

    @pl.when(n == pl.num_programs(1) - 1)
    def _():
        st_ref[...] = s_ref[...]


def _retention(qkvg, cos, sin, s0, batch, n_chunks, c, n_valid, out_dtype):
    m = qkvg.shape[0]
    colblk = lambda j: pl.BlockSpec((c, MIX_W), lambda b, n: (b * n_chunks + n, j))
    table = pl.BlockSpec((c, HEAD_DIM), lambda b, n: (n, 0))
    state = pl.BlockSpec((None, N_HEADS, HEAD_DIM, HEAD_DIM), lambda b, n: (b, 0, 0, 0))
    return pl.pallas_call(
        functools.partial(_ret_kernel, c=c, n_valid=n_valid),
        out_shape=(jax.ShapeDtypeStruct((m, MIX_W), out_dtype),
                   jax.ShapeDtypeStruct((batch, N_HEADS, HEAD_DIM, HEAD_DIM), F32)),
        grid=(batch, n_chunks),
        in_specs=[colblk(0), colblk(1), colblk(2), colblk(3), table, table, state],
        out_specs=(colblk(0), state),
        scratch_shapes=[pltpu.VMEM((N_HEADS, HEAD_DIM, HEAD_DIM), F32)],
        compiler_params=_params(("arbitrary", "arbitrary")),
        name="retention",
    )(qkvg, qkvg, qkvg, qkvg, cos, sin, s0)


def _merge_kernel(fox_ref, gdn_ref, ret_ref, w_ref, g0_ref, g1_ref, g2_ref, o_ref, w16_ref):
    @pl.when(pl.program_id(1) == 0)
    def _():
        w16_ref[...] = w_ref[...].astype(BF16)

    acc = _sigmoid(g0_ref[...]) * _dot(fox_ref[...], w16_ref[0])
    acc = acc + _sigmoid(g1_ref[...]) * _dot(gdn_ref[...], w16_ref[1])
    acc = acc + _sigmoid(g2_ref[...]) * _dot(ret_ref[...], w16_ref[2])
    o_ref[...] = acc.astype(o_ref.dtype)


def _merge(o_fox, o_gdn, o_ret, w_branch, layer, gate, tm, tn):
    m = o_fox.shape[0]
    d = w_branch.shape[3]
    nj = d // tn
    br = pl.BlockSpec((tm, MIX_W), lambda j, i: (i, 0))
    gspec = lambda b: pl.BlockSpec((tm, tn), lambda j, i: (i, b * nj + j))
    return pl.pallas_call(
        _merge_kernel,
        out_shape=jax.ShapeDtypeStruct((m, d), BF16),
        grid=(nj, m // tm),
        in_specs=[br, br, br,
                  pl.BlockSpec((None, N_BRANCH, MIX_W, tn), lambda j, i: (layer, 0, 0, j)),
                  gspec(0), gspec(1), gspec(2)],
        out_specs=pl.BlockSpec((tm, tn), lambda j, i: (i, j)),
        scratch_shapes=[pltpu.VMEM((N_BRANCH, MIX_W, tn), BF16)],
        compiler_params=_params(("arbitrary", "arbitrary")),
        name="branch_merge",
    )(o_fox, o_gdn, o_ret, w_branch, gate, gate, gate)


def _router_kernel(h_ref, whi_ref, wlo_ref, b_ref, idx_ref, wt_ref, cnt_ref, carry_ref):
    @pl.when(pl.program_id(0) == 0)
    def _():
        carry_ref[...] = jnp.zeros_like(carry_ref)

    h_hi, h_lo = _split2(h_ref[...])
    logits = _dot(h_hi, whi_ref[...]) + (_dot(h_hi, wlo_ref[...]) + _dot(h_lo, whi_ref[...])) + b_ref[...]
    tm = logits.shape[0]
    lane = _iota2(logits.shape, 1)
    lane_f = lane.astype(F32)
    logits = jnp.where(lane < N_EXPERTS, logits, -jnp.inf)
    vals, sels = [], []
    for k in range(TOP_K):
        top = jnp.max(logits, axis=1, keepdims=True)
        sel = jnp.min(jnp.where(logits == top, lane_f, float(LANES)), axis=1, keepdims=True)
        vals.append(top)
        sels.append(sel)
        logits = jnp.where(lane_f == sel, -jnp.inf, logits)
    chosen = [jnp.where(lane_f == sel, 1.0, 0.0) for sel in sels]
    picks = chosen[0] + chosen[1] + chosen[2] + chosen[3]
    earlier_rows = jnp.where(_iota2((tm, tm), 1) < _iota2((tm, tm), 0), 1.0, 0.0).astype(BF16)
    before = _dot(earlier_rows, picks.astype(BF16)) + carry_ref[0:1, :]
    carry_ref[0:1, :] = carry_ref[0:1, :] + jnp.sum(picks, axis=0, keepdims=True)
    es = [jnp.exp(v - vals[0]) for v in vals]
    tot = es[0] + es[1] + es[2] + es[3]
    idx = jnp.zeros(logits.shape, F32)
    wt = jnp.zeros(logits.shape, F32)
    for k in range(TOP_K):
        rank = jnp.sum(before * chosen[k], axis=1, keepdims=True)
        idx = jnp.where(lane == k, sels[k], jnp.where(lane == TOP_K + k, rank, idx))
        wt = jnp.where(lane == k, es[k] / tot, wt)
    idx_ref[...] = idx.astype(jnp.int32)
    wt_ref[...] = wt
    cnt_ref[...] = carry_ref[...]


def _router(h, m, w_hi, w_lo, b, tm):
    d = h.shape[1]
    tok = pl.BlockSpec((tm, LANES), lambda i: (i, 0))
    const = lambda rows: pl.BlockSpec((rows, LANES), lambda i: (0, 0))
    return pl.pallas_call(
        _router_kernel,
        out_shape=(jax.ShapeDtypeStruct((m, LANES), jnp.int32), jax.ShapeDtypeStruct((m, LANES), F32),
                   jax.ShapeDtypeStruct((SUBLANES, LANES), F32)),
        grid=(m // tm,),
        in_specs=[pl.BlockSpec((tm, d), lambda i: (i, 0)), const(d), const(d), const(1)],
        out_specs=(tok, tok, const(SUBLANES)),
        scratch_shapes=[pltpu.VMEM((SUBLANES, LANES), F32)],
        compiler_params=_params(("arbitrary",)),
        name="moe_router_topk",
    )(h, w_hi, w_lo, b)


def _row_copy(src_hbm, src_row, dst_vmem, dst_row, sem):
    return pltpu.make_async_copy(src_hbm.at[pl.ds(src_row, 1), :], dst_vmem.at[pl.ds(dst_row, 1), :], sem)


def _gather_kernel(tok_ref, nv_ref, h_ref, o_ref, buf_ref, sems):
    b = pl.program_id(0)
    blk = buf_ref.shape[1]
    slot = b % 2

    def fetch(block, to_slot):
        def issue(r, carry):
            _row_copy(h_ref, tok_ref[block * blk + r], buf_ref.at[to_slot], r, sems.at[to_slot]).start()
            return carry

        lax.fori_loop(0, blk, issue, 0, unroll=8)

    @pl.when(b == 0)
    def _():
        fetch(0, 0)

    @pl.when(b + 1 < nv_ref[0])
    def _():
        fetch(b + 1, 1 - slot)

    @pl.when(b < nv_ref[0])
    def _():
        def wait(r, carry):
            _row_copy(h_ref, 0, buf_ref.at[slot], r, sems.at[slot]).wait()
            return carry

        lax.fori_loop(0, blk, wait, 0, unroll=8)
        o_ref[...] = buf_ref[slot].astype(o_ref.dtype)

    @pl.when(b >= nv_ref[0])
    def _():
        o_ref[...] = jnp.zeros_like(o_ref)


def _gather_rows(slot_tok, n_valid, h_ext, n_blk, blk):
    d = h_ext.shape[1]
    return pl.pallas_call(
        _gather_kernel,
        out_shape=jax.ShapeDtypeStruct((n_blk * blk, d), BF16),
        grid_spec=pltpu.PrefetchScalarGridSpec(
            num_scalar_prefetch=2,
            grid=(n_blk,),
            in_specs=[pl.BlockSpec(memory_space=pl.ANY)],
            out_specs=pl.BlockSpec((blk, d), lambda b, tok, nv: (b, 0)),
            scratch_shapes=[pltpu.VMEM((2, blk, d), F32), pltpu.SemaphoreType.DMA((2,))],
        ),
        compiler_params=_params(("arbitrary",)),
        name="moe_token_gather",
    )(slot_tok, n_valid, h_ext)


def _stream_expert_weights(be_ref, first_ref, nxt_ref, copies, stage_ref, w16_ref):
    j = pl.program_id(0)
    b = pl.program_id(1)

    @pl.when(first_ref[b] == 1)
    def _():
        @pl.when(jnp.logical_and(j == 0, b == 0))
        def _():
            for c in copies(be_ref[0], 0):
                c.start()

        for c in copies(be_ref[b], j):
            c.wait()
        w16_ref[...] = stage_ref[...].astype(BF16)
        nxt = nxt_ref[b]

        @pl.when(nxt >= 0)
        def _():
            for c in copies(nxt, j):
                c.start()

        @pl.when(jnp.logical_and(nxt < 0, j + 1 < pl.num_programs(0)))
        def _():
            for c in copies(be_ref[0], j + 1):
                c.start()


def _gmm1_kernel(be_ref, nv_ref, first_ref, nxt_ref, x_ref, w_hbm, bg_ref, bl_ref, o_ref,
                 stage_ref, w16_ref, sems, *, layer, tn, d_ff):
    b = pl.program_id(1)

    def copies(e, sweep):
        return [pltpu.make_async_copy(
            w_hbm.at[layer, e, :, pl.ds(pl.multiple_of(half * d_ff + sweep * tn, tn), tn)],
            stage_ref.at[half], sems.at[half]) for half in range(2)]

    _stream_expert_weights(be_ref, first_ref, nxt_ref, copies, stage_ref, w16_ref)

    @pl.when(b < nv_ref[0])
    def _():
        x = x_ref[...]
        glu = jnp.minimum(_dot(x, w16_ref[0]) + bg_ref[...], SWIGLU_LIMIT)
        lin = jnp.clip(_dot(x, w16_ref[1]) + bl_ref[...], -SWIGLU_LIMIT, SWIGLU_LIMIT)
        o_ref[...] = (glu * _sigmoid(SWIGLU_ALPHA * glu) * (lin + 1.0)).astype(o_ref.dtype)

    @pl.when(b >= nv_ref[0])
    def _():
        o_ref[...] = jnp.zeros_like(o_ref)


def _gmm1(sched, xs, w_e1, b_e1, layer, n_blk, blk, tn):
    depth, n_exp, d, two_f = w_e1.shape
    d_ff = two_f // 2
    nj = d_ff // tn
    bspec = lambda off: pl.BlockSpec((None, None, 1, tn), lambda j, b, be, nv, fi, nx: (layer, be[b], 0, off + j))
    bias = b_e1.reshape(depth, n_exp, 1, two_f)
    return pl.pallas_call(
        functools.partial(_gmm1_kernel, layer=layer, tn=tn, d_ff=d_ff),
        out_shape=jax.ShapeDtypeStruct((n_blk * blk, d_ff), BF16),
        grid_spec=pltpu.PrefetchScalarGridSpec(
            num_scalar_prefetch=4,
            grid=(nj, n_blk),
            in_specs=[
                pl.BlockSpec((blk, d), lambda j, b, be, nv, fi, nx: (jnp.minimum(b, nv[0] - 1), 0)),
                pl.BlockSpec(memory_space=pl.ANY),
                bspec(0), bspec(nj),
            ],
            out_specs=pl.BlockSpec((blk, tn), lambda j, b, be, nv, fi, nx: (b, j)),
            scratch_shapes=[pltpu.VMEM((2, d, tn), F32), pltpu.VMEM((2, d, tn), BF16),
                            pltpu.SemaphoreType.DMA((2,))],
        ),
        compiler_params=_params(("arbitrary", "arbitrary")),
        name="moe_expert_up",
    )(*sched, xs, w_e1, bias, bias)


def _gmm2_kernel(be_ref, nv_ref, first_ref, nxt_ref, a_ref, w_hbm, b_ref, o_ref, stage_ref, w16_ref, sems,
                 *, layer, tn):
    b = pl.program_id(1)

    def copies(e, sweep):
        return [pltpu.make_async_copy(w_hbm.at[layer, e, :, pl.ds(pl.multiple_of(sweep * tn, tn), tn)],
                                      stage_ref, sems.at[0])]

    _stream_expert_weights(be_ref, first_ref, nxt_ref, copies, stage_ref, w16_ref)

    @pl.when(b < nv_ref[0])
    def _():
        o_ref[...] = _dot(a_ref[...], w16_ref[...]) + b_ref[...]

    @pl.when(b >= nv_ref[0])
    def _():
        o_ref[...] = jnp.zeros_like(o_ref)


def _gmm2(sched, act, w_e2, b_e2, layer, n_blk, blk, tn):
    depth, n_exp, d_ff, d = w_e2.shape
    return pl.pallas_call(
        functools.partial(_gmm2_kernel, layer=layer, tn=tn),
        out_shape=jax.ShapeDtypeStruct((n_blk * blk, d), F32),
        grid_spec=pltpu.PrefetchScalarGridSpec(
            num_scalar_prefetch=4,
            grid=(d // tn, n_blk),
            in_specs=[
                pl.BlockSpec((blk, d_ff), lambda j, b, be, nv, fi, nx: (jnp.minimum(b, nv[0] - 1), 0)),
                pl.BlockSpec(memory_space=pl.ANY),
                pl.BlockSpec((None, None, 1, tn), lambda j, b, be, nv, fi, nx: (layer, be[b], 0, j)),
            ],
            out_specs=pl.BlockSpec((blk, tn), lambda j, b, be, nv, fi, nx: (b, j)),
            scratch_shapes=[pltpu.VMEM((d_ff, tn), F32), pltpu.VMEM((d_ff, tn), BF16),
                            pltpu.SemaphoreType.DMA((1,))],
        ),
        compiler_params=_params(("arbitrary", "arbitrary")),
        name="moe_expert_down",
    )(*sched, act, w_e2, b_e2.reshape(depth, n_exp, 1, d))


def _combine_kernel(pos_ref, ys_ref, wt_ref, x_ref, gate_ref, o_ref, buf_ref, sems):
    tm = x_ref.shape[0]
    i = pl.program_id(0)
    slot = i % 2

    def fetch(tile, to_slot):
        base = tile * tm * TOP_K

        def issue(t, carry):
            for k in range(TOP_K):
                _row_copy(ys_ref, pos_ref[base + t * TOP_K + k], buf_ref.at[to_slot, k], t, sems.at[to_slot]).start()
            return carry

        lax.fori_loop(0, tm, issue, 0, unroll=2)

    @pl.when(i == 0)
    def _():
        fetch(0, 0)

    @pl.when(i + 1 < pl.num_programs(0))
    def _():
        fetch(i + 1, 1 - slot)

    def wait(t, carry):
        for k in range(TOP_K):
            _row_copy(ys_ref, 0, buf_ref.at[slot, k], t, sems.at[slot]).wait()
        return carry

    lax.fori_loop(0, tm, wait, 0, unroll=2)
    wt = wt_ref[...]
    acc = buf_ref[slot, 0] * wt[:, 0:1]
    for k in range(1, TOP_K):
        acc = acc + buf_ref[slot, k] * wt[:, k:k + 1]
    o_ref[...] = x_ref[...] + gate_ref[...] * acc


def _combine(pos, ys, wt, x, gate, tm, rows_per_group):
    m, d = x.shape
    r = gate.shape[1]
    tiles = rows_per_group // tm
    return pl.pallas_call(
        _combine_kernel,
        out_shape=jax.ShapeDtypeStruct((m, d), F32),
        grid_spec=pltpu.PrefetchScalarGridSpec(
            num_scalar_prefetch=1,
            grid=(m // tm,),
            in_specs=[
                pl.BlockSpec(memory_space=pl.ANY),
                pl.BlockSpec((tm, LANES), lambda i, pos: (i, 0)),
                pl.BlockSpec((tm, d), lambda i, pos: (i, 0)),
                pl.BlockSpec((None, r, d), lambda i, pos: (i // tiles, 0, 0)),
            ],
            out_specs=pl.BlockSpec((tm, d), lambda i, pos: (i, 0)),
            scratch_shapes=[pltpu.VMEM((2, TOP_K, tm, d), F32), pltpu.SemaphoreType.DMA((2,))],
        ),
        compiler_params=_params(("arbitrary",)),
        name="moe_combine",
    )(pos, ys, wt, x, gate)


TM = 512
TN = 1024


def _pad_lanes(a):
    return jnp.pad(a, ((0, 0), (0, LANES - a.shape[1])))


def _layer_weights(l, d, w_in, fox_f_bias, gdn_a_log, gdn_dt_bias, w_router, b_router):
    w = w_in[l]
    o_f = 3 * MIX_W
    o_gqkv = o_f + N_HEADS
    o_ga = o_gqkv + 3 * MIX_W
    o_gb = o_ga + N_HEADS
    o_gz = o_gb + N_HEADS
    o_ret = o_gz + MIX_W
    o_gate = o_ret + 4 * MIX_W
    small = _pad_lanes(jnp.concatenate([w[:, o_f:o_gqkv], w[:, o_ga:o_gb], w[:, o_gb:o_gz]], axis=1))
    ws_hi, ws_lo = _split2(small)
    prm = jnp.zeros((SUBLANES, LANES), F32)
    prm = prm.at[0, 0:8].set(fox_f_bias[l]).at[0, 8:16].set(gdn_dt_bias[l])
    prm = prm.at[1, 8:16].set(-jnp.exp(gdn_a_log[l]))
    wr_hi, wr_lo = _split2(_pad_lanes(w_router[l]))
    return dict(
        w_fox=w[:, 0:o_f].astype(BF16),
        w_gqkv=w[:, o_gqkv:o_ga].astype(BF16),
        w_gz=w[:, o_gz:o_ret].astype(BF16),
        w_ret=w[:, o_ret:o_gate].astype(BF16),
        w_gate=w[:, o_gate:o_gate + N_BRANCH * d].astype(BF16),
        ws_hi=ws_hi, ws_lo=ws_lo, prm=prm, wr_hi=wr_hi, wr_lo=wr_lo,
        b_router=_pad_lanes(b_router[l].reshape(1, N_EXPERTS)),
    )


def _rope_tables(pos):
    half = HEAD_DIM // 2
    inv = ROPE_BASE ** (-jnp.arange(half, dtype=F32) / half)
    ang = pos.astype(F32)[:, None] * inv[None, :]
    cos, sin = jnp.cos(ang), jnp.sin(ang)
    return jnp.concatenate([cos, cos], axis=1), jnp.concatenate([-sin, sin], axis=1)


def _project(h, lw, tm):
    fox = _mm(h, lw['w_fox'], tm, TN)
    gqkv = _mm(h, lw['w_gqkv'], tm, TN)
    gz = _mm(h, lw['w_gz'], tm, TN)
    ret = _mm(h, lw['w_ret'], tm, TN)
    gate = _mm(h, lw['w_gate'], tm, TN)
    gates = _gates(h, lw['ws_hi'], lw['ws_lo'], lw['prm'], tm)
    return fox, gqkv, gz, ret, gate, gates


def _prompt_mixer(x, mod, lw, p, batch, seq):
    m, d = x.shape
    sh1, sc1, g1, sh2, sc2, g2 = mod
    h = _norm_mod(x, p['norm1'], sc1, sh1, TM, seq)
    fox, gqkv, gz, ret, gate, gates = _project(h, lw, TM)
    cum = _cumsum_rows(gates, 256, seq)
    fq16, fk, fk16, fv16 = _fox_prep(fox, p['fox_qn'], p['fox_kn'], TM)
    cum_t = cum[:, :N_HEADS].reshape(batch, seq, N_HEADS).transpose(0, 2, 1)
    o_fox = _flash_attention(fq16, fk16, fv16, cum_t[..., None], cum_t[:, :, None, :], batch, seq, 512)
    n_chunks = seq // CHUNK
    gates_t = gates.reshape(m // CHUNK, CHUNK, LANES).transpose(0, 2, 1)
    zero_state = jnp.zeros((batch, N_HEADS, HEAD_DIM, HEAD_DIM), F32)
    o_gdn, s_gdn = _gdn(gqkv, p['gdn_conv'], jnp.zeros((batch, SUBLANES, 3 * MIX_W), F32), gates, gates_t, gz,
                        p['gdn_nw'], zero_state, batch, n_chunks, CHUNK, BF16)
    cos, sin = _rope_tables(jnp.arange(seq, dtype=jnp.int32))
    o_ret, s_ret = _retention(ret, cos, sin, zero_state, batch, n_chunks, CHUNK, CHUNK, BF16)
    merged = _merge(o_fox, o_gdn, o_ret, p['w_branch'], p['layer'], gate, TM, 512)
    x1 = _mm_residual(merged, p['w_out'], p['layer'], x, g1, TM, TN, seq)
    h2 = _norm_mod(x1, p['norm2'], sc2, sh2, TM, seq, F32)
    state = (fk.reshape(batch, seq, N_HEADS, HEAD_DIM),
             fox[:, 2 * MIX_W:].reshape(batch, seq, N_HEADS, HEAD_DIM),
             gates[:, :N_HEADS].reshape(batch, seq, N_HEADS),
             s_gdn,
             gqkv.reshape(batch, seq, 3 * MIX_W)[:, seq - (CONV_W - 1):],
             s_ret)
    return x1, h2, state


def _sample_mixer(x, mod, lw, p, past, layer, batch, seq, past_len):
    m, d = x.shape
    sh1, sc1, g1, sh2, sc2, g2 = mod
    n_pages = past_len // PAGE_SIZE
    h = _norm_mod(x, p['norm1'], sc1, sh1, m, m)
    fox, gqkv, gz, ret, gate, gates = _project(h, lw, m)
    cum = _cumsum_rows(gates, m, seq)
    fq16, fk, fk16, fv16 = _fox_prep(fox, p['fox_qn'], p['fox_kn'], m)

    pad_q = SUBLANES - seq
    def head_major(a, pad_to):
        a = a.reshape(batch, seq, N_HEADS, HEAD_DIM).transpose(0, 2, 1, 3)
        return jnp.pad(a, ((0, 0), (0, 0), (0, pad_to - seq), (0, 0)))

    q64 = head_major(fq16, SUBLANES).reshape(batch, N_HEADS * SUBLANES, HEAD_DIM).astype(F32)
    cum_t = cum[:, :N_HEADS].reshape(batch, seq, N_HEADS).transpose(0, 2, 1)
    cq = jnp.pad(cum_t, ((0, 0), (0, 0), (0, pad_q))).reshape(batch, N_HEADS * SUBLANES, 1)
    kn = head_major(fk16, LANES).reshape(batch, N_HEADS * LANES, HEAD_DIM)
    vn = head_major(fv16, LANES).reshape(batch, N_HEADS * LANES, HEAD_DIM)
    ckn = jnp.pad(cum_t, ((0, 0), (0, 0), (0, LANES - seq)))
    bias = _suffix_sums(past['pt'], past['logf_t'], layer, batch, n_pages)
    o = _decode_attention(past['pt'], q64, past['k'], past['v'], bias, cq, kn, vn, ckn, layer, batch, n_pages)
    o_fox = (o.reshape(batch, N_HEADS, SUBLANES, HEAD_DIM)[:, :, :seq].transpose(0, 2, 1, 3)
             .reshape(m, MIX_W).astype(BF16))

    def pad_rows(a):
        return jnp.pad(a.reshape(batch, seq, -1), ((0, 0), (0, pad_q), (0, 0))).reshape(batch * SUBLANES, -1)

    def unpad_rows(a):
        return a.reshape(batch, SUBLANES, -1)[:, :seq].reshape(m, -1)

    gates_p = pad_rows(gates)
    gates_t = gates_p.reshape(batch, SUBLANES, LANES).transpose(0, 2, 1)
    conv0 = jnp.pad(past['conv'], ((0, 0), (SUBLANES - (CONV_W - 1), 0), (0, 0)))
    o_gdn, s_gdn = _gdn(pad_rows(gqkv), p['gdn_conv'], conv0, gates_p, gates_t, pad_rows(gz), p['gdn_nw'],
                        past['gdn'], batch, 1, SUBLANES, F32)
    cos, sin = _rope_tables(past_len + jnp.arange(seq, dtype=jnp.int32))
    cos = jnp.pad(cos, ((0, pad_q), (0, 0)))
    sin = jnp.pad(sin, ((0, pad_q), (0, 0)))
    o_ret, s_ret = _retention(pad_rows(ret), cos, sin, past['ret'], batch, 1, SUBLANES, seq, F32)
    o_gdn = unpad_rows(o_gdn).astype(BF16)
    o_ret = unpad_rows(o_ret).astype(BF16)

    merged = _merge(o_fox, o_gdn, o_ret, p['w_branch'], p['layer'], gate, m, 512)
    x1 = _mm_residual(merged, p['w_out'], p['layer'], x, g1, m, TN, m)
    h2 = _norm_mod(x1, p['norm2'], sc2, sh2, m, m, F32)
    state = (fk.reshape(batch, seq, N_HEADS, HEAD_DIM),
             fox[:, 2 * MIX_W:].reshape(batch, seq, N_HEADS, HEAD_DIM),
             gates[:, :N_HEADS].reshape(batch, seq, N_HEADS),
             s_gdn,
             gqkv.reshape(batch, seq, 3 * MIX_W)[:, seq - (CONV_W - 1):],
             s_ret)
    return x1, h2, state


def _moe(h_ext, n, lw, w_e1, b_e1, w_e2, b_e2, layer):
    idx, wt, cnt = _router(h_ext, n, lw['wr_hi'], lw['wr_lo'], lw['b_router'], LANES)
    flat_e = idx[:, :TOP_K].reshape(-1)
    rank = idx[:, TOP_K:2 * TOP_K].reshape(-1)
    counts = cnt[0, :N_EXPERTS].astype(jnp.int32)
    n_assign = n * TOP_K
    padded = (counts + MOE_BLK - 1) // MOE_BLK * MOE_BLK
    pad_end = jnp.cumsum(padded)
    dest = ((pad_end - padded)[flat_e] + rank).astype(jnp.int32)
    n_blk = -(-n_assign // MOE_BLK) + N_EXPERTS
    slot_tok = jnp.full((n_blk * MOE_BLK,), n, jnp.int32).at[dest].set(
        jnp.arange(n_assign, dtype=jnp.int32) // TOP_K)
    blk_ids = jnp.arange(n_blk, dtype=jnp.int32)
    blk_e = jnp.sum((blk_ids[:, None] * MOE_BLK >= pad_end[None, :]).astype(jnp.int32), axis=1)
    n_valid = (pad_end[-1] // MOE_BLK).astype(jnp.int32)
    blk_e = jnp.where(blk_ids < n_valid, jnp.minimum(blk_e, N_EXPERTS - 1), 0)
    blk_e = jnp.where(blk_ids < n_valid, blk_e, jnp.max(blk_e)).astype(jnp.int32)
    prev_e = jnp.concatenate([jnp.full((1,), -1, jnp.int32), blk_e[:-1]])
    first = jnp.where(blk_ids < n_valid, blk_e != prev_e, False).astype(jnp.int32)
    experts = jnp.arange(N_EXPERTS, dtype=jnp.int32)
    later = jnp.where((experts[None, :] > blk_e[:, None]) & (counts[None, :] > 0), experts[None, :], N_EXPERTS)
    nxt = jnp.min(later, axis=1)
    nxt = jnp.where(nxt < N_EXPERTS, nxt, -1).astype(jnp.int32)
    n_valid = n_valid.reshape(1)
    sched = (blk_e, n_valid, first, nxt)
    xs = _gather_rows(slot_tok, n_valid, h_ext, n_blk, MOE_BLK)
    tn = min(MOE_TN, w_e2.shape[2], w_e2.shape[3])
    act = _gmm1(sched, xs, w_e1, b_e1, layer, n_blk, MOE_BLK, tn)
    ys = _gmm2(sched, act, w_e2, b_e2, layer, n_blk, MOE_BLK, tn)
    return ys, dest, wt


def kernel(x_prompt, x_sample, cache_fox_k, cache_fox_v, cache_fox_logf, state_gdn, state_gdn_conv, state_ret,
           page_table, c_prompt, c_sample, norm1_w, norm2_w, w_ada, b_ada, w_in, fox_f_bias, fox_qn_w, fox_kn_w,
           gdn_conv_w, gdn_a_log, gdn_dt_bias, gdn_norm_w, w_branch, w_out, w_router, b_router,
           w_e1, b_e1, w_e2, b_e2):
    batch, seq, d = x_prompt.shape
    dec_batch, dec_seq, _ = x_sample.shape
    depth = w_in.shape[0]
    n_pool = cache_fox_k.shape[1]
    n_pages = page_table.shape[1]
    past_len = n_pages * PAGE_SIZE
    mp = batch * seq
    ms = dec_batch * dec_seq

    n_cond = batch + dec_batch
    c_all = jnp.concatenate([c_prompt, c_sample, jnp.zeros((-n_cond % SUBLANES, d), F32)], axis=0)
    mod_all = _modulation(c_all, w_ada, b_ada)

    cache_k = cache_fox_k.reshape(depth, n_pool, PAGE_SIZE * N_HEADS, HEAD_DIM)
    cache_v = cache_fox_v.reshape(depth, n_pool, PAGE_SIZE * N_HEADS, HEAD_DIM)
    logf_t = cache_fox_logf.transpose(0, 1, 3, 2)
    pt_flat = page_table.reshape(-1).astype(jnp.int32)

    xp = x_prompt.reshape(mp, d)
    xs = x_sample.reshape(ms, d)
    st_p, st_s = [], []
    for l in range(depth):
        lw = _layer_weights(l, d, w_in, fox_f_bias, gdn_a_log, gdn_dt_bias, w_router, b_router)
        p = dict(norm1=norm1_w[l], norm2=norm2_w[l], fox_qn=fox_qn_w[l], fox_kn=fox_kn_w[l],
                 gdn_conv=gdn_conv_w[l], gdn_nw=gdn_norm_w[l], w_branch=w_branch, w_out=w_out, layer=l)
        mod_p = [t[:, None, :] for t in jnp.split(mod_all[l, :batch], 6, axis=-1)]
        mod_s = [t[None] for t in jnp.split(jnp.repeat(mod_all[l, batch:n_cond], dec_seq, axis=0), 6, axis=-1)]
        past = dict(pt=pt_flat, k=cache_k, v=cache_v, logf_t=logf_t, gdn=state_gdn[l],
                    conv=state_gdn_conv[l], ret=state_ret[l])
        x1p, h2p, sp = _prompt_mixer(xp, mod_p, lw, p, batch, seq)
        x1s, h2s, ss = _sample_mixer(xs, mod_s, lw, p, past, l, dec_batch, dec_seq, past_len)
        h_ext = jnp.concatenate([h2p, h2s, jnp.zeros((SUBLANES, d), F32)], axis=0)
        ys, dest, wt = _moe(h_ext, mp + ms, lw, w_e1, b_e1, w_e2, b_e2, l)
        xp = _combine(dest[:mp * TOP_K], ys, wt[:mp], x1p, mod_p[5], LANES, seq)
        xs = _combine(dest[mp * TOP_K:], ys, wt[mp:], x1s, mod_s[5], LANES, ms)
        st_p.append(sp)
        st_s.append(ss)

    def stk(sts, i):
        return jnp.stack([s[i] for s in sts])

    return (xp.reshape(batch, seq, d), xs.reshape(dec_batch, dec_seq, d),
            stk(st_p, 0), stk(st_p, 1), stk(st_p, 2), stk(st_p, 3), stk(st_p, 4), stk(st_p, 5),
            stk(st_s, 0), stk(st_s, 1), stk(st_s, 2), stk(st_s, 3), stk(st_s, 4), stk(st_s, 5))
```

```python
import functools
import math

import jax
import jax.numpy as jnp
from jax import lax
from jax.experimental import pallas as pl
from jax.experimental.pallas import tpu as pltpu

F32 = jnp.float32
BF16 = jnp.bfloat16

HEAD_DIM = 128
N_HEADS = 8
MIX_W = N_HEADS * HEAD_DIM
N_BRANCH = 3
CONV_W = 4
CHUNK = 64
PAGE_SIZE = 128
ROPE_BASE = 10000.0
N_EXPERTS = 32
TOP_K = 4
SWIGLU_LIMIT = 7.0
SWIGLU_ALPHA = 1.702
EPS = 1e-6

LANES = 128
SUBLANES = 8
VMEM_LIMIT = 48 * 1024 * 1024
MOE_VMEM_LIMIT = 56 * 1024 * 1024
MOE_BLK = 256
MOE_TN = 1024
NEG_BIG = -1e30


def _params(sem, vmem=VMEM_LIMIT):
    return pltpu.CompilerParams(dimension_semantics=sem, vmem_limit_bytes=vmem)


def _dot(a, b):
    return jnp.dot(a, b, preferred_element_type=F32)


def _dot_nt(a, b):
    return lax.dot_general(a, b, (((1,), (1,)), ((), ())), preferred_element_type=F32)


def _dot_tn(a, b):
    return lax.dot_general(a, b, (((0,), (0,)), ((), ())), preferred_element_type=F32)


def _split2(a):
    hi = a.astype(BF16)
    lo = (a - hi.astype(F32)).astype(BF16)
    return hi, lo


def _split3(a):
    hi = a.astype(BF16)
    r = a - hi.astype(F32)
    mid = r.astype(BF16)
    lo = (r - mid.astype(F32)).astype(BF16)
    return hi, mid, lo


def _dot_exact_lhs(mask_bf16, x):
    hi, mid, lo = _split3(x)
    return _dot(mask_bf16, hi) + (_dot(mask_bf16, mid) + _dot(mask_bf16, lo))


def _dot_exact_rhs(x, mask_bf16):
    hi, mid, lo = _split3(x)
    return _dot(hi, mask_bf16) + (_dot(mid, mask_bf16) + _dot(lo, mask_bf16))


def _sigmoid(x):
    return 1.0 / (1.0 + jnp.exp(-x))


def _silu(x):
    return x * _sigmoid(x)


def _softplus(x):
    return jnp.maximum(x, 0.0) + jnp.log1p(jnp.exp(-jnp.abs(x)))


def _iota2(shape, dim):
    return lax.broadcasted_iota(jnp.int32, shape, dim)


def _modulation_kernel(c_ref, w_ref, b_ref, o_ref):
    c = _silu(c_ref[...])
    c_hi, c_lo = _split2(c)
    w16 = w_ref[...].astype(BF16)
    o_ref[...] = _dot(c_hi, w16) + _dot(c_lo, w16) + b_ref[...]


def _modulation(c_all, w_ada, b_ada, tn=1024):
    depth, d, n = w_ada.shape
    rows = c_all.shape[0]
    return pl.pallas_call(
        _modulation_kernel,
        out_shape=jax.ShapeDtypeStruct((depth, rows, n), F32),
        grid=(depth, n // tn),
        in_specs=[
            pl.BlockSpec((rows, d), lambda l, j: (0, 0)),
            pl.BlockSpec((None, d, tn), lambda l, j: (l, 0, j)),
            pl.BlockSpec((None, 1, tn), lambda l, j: (l, 0, j)),
        ],
        out_specs=pl.BlockSpec((None, rows, tn), lambda l, j: (l, 0, j)),
        compiler_params=_params(("arbitrary", "arbitrary")),
        name="adaln_modulation",
    )(c_all, w_ada, b_ada.reshape(depth, 1, n))


def _rms(x):
    return x * lax.rsqrt(jnp.mean(x * x, axis=-1, keepdims=True) + EPS)


def _norm_mod_kernel(x_ref, w_ref, sc_ref, sh_ref, o_ref):
    y = _rms(x_ref[...]) * w_ref[...]
    o_ref[...] = (y * (1.0 + sc_ref[...]) + sh_ref[...]).astype(o_ref.dtype)


def _mod_spec(mod, tm, rows_per_group):
    r, d = mod.shape[1], mod.shape[2]
    tiles = rows_per_group // tm
    return pl.BlockSpec((None, r, d), lambda i: (i // tiles, 0, 0))


def _norm_mod(x, w, sc, sh, tm, rows_per_group, out_dtype=BF16):
    m, d = x.shape
    return pl.pallas_call(
        _norm_mod_kernel,
        out_shape=jax.ShapeDtypeStruct((m, d), out_dtype),
        grid=(m // tm,),
        in_specs=[
            pl.BlockSpec((tm, d), lambda i: (i, 0)),
            pl.BlockSpec((1, d), lambda i: (0, 0)),
            _mod_spec(sc, tm, rows_per_group),
            _mod_spec(sh, tm, rows_per_group),
        ],
        out_specs=pl.BlockSpec((tm, d), lambda i: (i, 0)),
        compiler_params=_params(("arbitrary",)),
        name="norm_modulate",
    )(x, w.reshape(1, d), sc, sh)


def _norm_mod_into_kernel(buf_ref, x_ref, w_ref, sc_ref, sh_ref, o_ref):
    del buf_ref
    _norm_mod_kernel(x_ref, w_ref, sc_ref, sh_ref, o_ref)


def _norm_mod_into(buf, x, w, sc, sh, tm, rows_per_group, first_block):
    m, d = x.shape
    return pl.pallas_call(
        _norm_mod_into_kernel,
        out_shape=jax.ShapeDtypeStruct(buf.shape, buf.dtype),
        grid=(m // tm,),
        in_specs=[
            pl.BlockSpec(memory_space=pl.ANY),
            pl.BlockSpec((tm, d), lambda i: (i, 0)),
            pl.BlockSpec((1, d), lambda i: (0, 0)),
            _mod_spec(sc, tm, rows_per_group),
            _mod_spec(sh, tm, rows_per_group),
        ],
        out_specs=pl.BlockSpec((tm, d), lambda i: (first_block + i, 0)),
        input_output_aliases={0: 0},
        compiler_params=_params(("arbitrary",)),
        name="norm_modulate_into",
    )(buf, x, w.reshape(1, d), sc, sh)


def _mm_kernel(a_ref, w_ref, o_ref, w16_ref):
    @pl.when(pl.program_id(1) == 0)
    def _():
        w16_ref[...] = w_ref[...].astype(BF16)

    o_ref[...] = _dot(a_ref[...], w16_ref[...]).astype(o_ref.dtype)


def _mm_residual_kernel(a_ref, w_ref, res_ref, gate_ref, o_ref, w16_ref):
    @pl.when(pl.program_id(1) == 0)
    def _():
        w16_ref[...] = w_ref[...].astype(BF16)

    o_ref[...] = res_ref[...] + gate_ref[...] * _dot(a_ref[...], w16_ref[...])


def _mm(a, w, tm, tn, out_dtype=F32):
    m, k = a.shape
    n = w.shape[1]
    return pl.pallas_call(
        _mm_kernel,
        out_shape=jax.ShapeDtypeStruct((m, n), out_dtype),
        grid=(n // tn, m // tm),
        in_specs=[
            pl.BlockSpec((tm, k), lambda j, i: (i, 0)),
            pl.BlockSpec((k, tn), lambda j, i: (0, j)),
        ],
        out_specs=pl.BlockSpec((tm, tn), lambda j, i: (i, j)),
        scratch_shapes=[pltpu.VMEM((k, tn), BF16)],
        compiler_params=_params(("arbitrary", "arbitrary")),
        name="matmul",
    )(a, w)


def _mm_residual(a, w, layer, res, gate, tm, tn, rows_per_group):
    m, k = a.shape
    n = w.shape[2]
    r = gate.shape[1]
    tiles = rows_per_group // tm
    return pl.pallas_call(
        _mm_residual_kernel,
        out_shape=jax.ShapeDtypeStruct((m, n), F32),
        grid=(n // tn, m // tm),
        in_specs=[
            pl.BlockSpec((tm, k), lambda j, i: (i, 0)),
            pl.BlockSpec((None, k, tn), lambda j, i: (layer, 0, j)),
            pl.BlockSpec((tm, tn), lambda j, i: (i, j)),
            pl.BlockSpec((None, r, tn), lambda j, i: (i // tiles, 0, j)),
        ],
        out_specs=pl.BlockSpec((tm, tn), lambda j, i: (i, j)),
        scratch_shapes=[pltpu.VMEM((k, tn), BF16)],
        compiler_params=_params(("arbitrary", "arbitrary")),
        name="matmul_residual",
    )(a, w, res, gate)


def _gates_kernel(h_ref, whi_ref, wlo_ref, prm_ref, o_ref):
    h = h_ref[...]
    z = _dot(h, whi_ref[...]) + _dot(h, wlo_ref[...]) + prm_ref[0:1, :]
    lane = _iota2(z.shape, 1)
    tail = jnp.log1p(jnp.exp(-jnp.abs(z)))
    lf = jnp.minimum(z, 0.0) - tail
    g = prm_ref[1:2, :] * (jnp.maximum(z, 0.0) + tail)
    beta = _sigmoid(z)
    o_ref[...] = jnp.where(lane < 8, lf, jnp.where(lane < 16, g, beta))


def _gates(h, w_hi, w_lo, prm, tm):
    m, d = h.shape
    return pl.pallas_call(
        _gates_kernel,
        out_shape=jax.ShapeDtypeStruct((m, LANES), F32),
        grid=(m // tm,),
        in_specs=[
            pl.BlockSpec((tm, d), lambda i: (i, 0)),
            pl.BlockSpec((d, LANES), lambda i: (0, 0)),
            pl.BlockSpec((d, LANES), lambda i: (0, 0)),
            pl.BlockSpec((SUBLANES, LANES), lambda i: (0, 0)),
        ],
        out_specs=pl.BlockSpec((tm, LANES), lambda i: (i, 0)),
        compiler_params=_params(("arbitrary",)),
        name="gate_projection",
    )(h, w_hi, w_lo, prm)


def _cumsum_kernel(x_ref, o_ref, carry_ref, *, group):
    tm = x_ref.shape[0]
    r = _iota2((tm, tm), 0)
    c = _iota2((tm, tm), 1)
    if group >= tm:
        tri = jnp.where(c <= r, 1.0, 0.0).astype(BF16)
    else:
        tri = jnp.where(c <= r, jnp.where(r // group == c // group, 1.0, 0.0), 0.0).astype(BF16)
    cs = _dot_exact_lhs(tri, x_ref[...])
    if group > tm:
        @pl.when(pl.program_id(0) % (group // tm) == 0)
        def _():
            carry_ref[...] = jnp.zeros_like(carry_ref)

        cs = cs + carry_ref[...]
        carry_ref[...] = cs[tm - 1:tm, :]
    o_ref[...] = cs


def _cumsum_rows(x, tm, group):
    m, n = x.shape
    return pl.pallas_call(
        functools.partial(_cumsum_kernel, group=group),
        out_shape=jax.ShapeDtypeStruct((m, n), F32),
        grid=(m // tm,),
        in_specs=[pl.BlockSpec((tm, n), lambda i: (i, 0))],
        out_specs=pl.BlockSpec((tm, n), lambda i: (i, 0)),
        scratch_shapes=[pltpu.VMEM((1, n), F32)],
        compiler_params=_params(("arbitrary",)),
        name="cumsum_rows",
    )(x)


def _fox_prep_kernel(q_ref, k_ref, v_ref, qw_ref, kw_ref, fq16_ref, fk_ref, fk16_ref, fv16_ref):
    for h in range(N_HEADS):
        sl = slice(h * HEAD_DIM, (h + 1) * HEAD_DIM)
        fq16_ref[:, sl] = (_rms(q_ref[:, sl]) * qw_ref[...]).astype(BF16)
        kn = _rms(k_ref[:, sl]) * kw_ref[...]
        fk_ref[:, sl] = kn
        fk16_ref[:, sl] = kn.astype(BF16)
    fv16_ref[...] = v_ref[...].astype(BF16)


def _fox_prep(qkv, qw, kw, tm):
    m = qkv.shape[0]
    col = lambda c: pl.BlockSpec((tm, MIX_W), lambda i: (i, c))
    wspec = pl.BlockSpec((1, HEAD_DIM), lambda i: (0, 0))
    return pl.pallas_call(
        _fox_prep_kernel,
        out_shape=(
            jax.ShapeDtypeStruct((m, MIX_W), BF16),
            jax.ShapeDtypeStruct((m, MIX_W), F32),
            jax.ShapeDtypeStruct((m, MIX_W), BF16),
            jax.ShapeDtypeStruct((m, MIX_W), BF16),
        ),
        grid=(m // tm,),
        in_specs=[col(0), col(1), col(2), wspec, wspec],
        out_specs=(col(0), col(0), col(0), col(0)),
        compiler_params=_params(("arbitrary",)),
        name="fox_qk_norm",
    )(qkv, qkv, qkv, qw.reshape(1, HEAD_DIM), kw.reshape(1, HEAD_DIM))


def _online_softmax_update(s, v16, m_ref, l_ref, acc_ref):
    m_prev = m_ref[...]
    m_new = jnp.maximum(m_prev, jnp.max(s, axis=1, keepdims=True))
    alpha = jnp.exp(m_prev - m_new)
    p = jnp.exp(s - m_new)
    l_ref[...] = alpha * l_ref[...] + jnp.sum(p, axis=1, keepdims=True)
    acc_ref[...] = alpha * acc_ref[...] + _dot(p.astype(BF16), v16)
    m_ref[...] = m_new


FLASH_HEADS = 2


def _flash_kernel(qi_ref, kj_ref, q_ref, k_ref, v_ref, cq_ref, ck_ref, o_ref, m_ref, l_ref, acc_ref, *, scale):
    i = qi_ref[pl.program_id(2)]
    j = kj_ref[pl.program_id(2)]

    @pl.when(j == 0)
    def _():
        m_ref[...] = jnp.full_like(m_ref, NEG_BIG)
        l_ref[...] = jnp.zeros_like(l_ref)
        acc_ref[...] = jnp.zeros_like(acc_ref)

    heads = range(m_ref.shape[0])
    cols = [slice(g * HEAD_DIM, (g + 1) * HEAD_DIM) for g in heads]

    def update(diagonal):
        ss = [_dot_nt(q_ref[:, cols[g]], k_ref[:, cols[g]]) * scale + (cq_ref[g] - ck_ref[g]) for g in heads]
        if diagonal:
            keep = _iota2(ss[0].shape, 0) >= _iota2(ss[0].shape, 1)
            ss = [jnp.where(keep, s, NEG_BIG) for s in ss]
        m_prev = [m_ref[g] for g in heads]
        m_new = [jnp.maximum(m_prev[g], jnp.max(ss[g], axis=1, keepdims=True)) for g in heads]
        alpha = [jnp.exp(m_prev[g] - m_new[g]) for g in heads]
        ps = [jnp.exp(ss[g] - m_new[g]) for g in heads]
        ls = [alpha[g] * l_ref[g] + jnp.sum(ps[g], axis=1, keepdims=True) for g in heads]
        accs = [alpha[g] * acc_ref[g] + _dot(ps[g].astype(BF16), v_ref[:, cols[g]]) for g in heads]
        for g in heads:
            m_ref[g] = m_new[g]
            l_ref[g] = ls[g]
            acc_ref[g] = accs[g]

    @pl.when(j < i)
    def _():
        update(False)

    @pl.when(j == i)
    def _():
        update(True)
        for g in heads:
            o_ref[:, cols[g]] = (acc_ref[g] / l_ref[g]).astype(o_ref.dtype)


def _flash_attention(fq16, fk16, fv16, cum_col, cum_row, batch, seq, t):
    n = seq // t
    pairs = [(i, j) for i in range(n) for j in range(i + 1)]
    qi = jnp.asarray([i for i, _ in pairs], jnp.int32)
    kj = jnp.asarray([j for _, j in pairs], jnp.int32)
    g = FLASH_HEADS
    qspec = pl.BlockSpec((t, g * HEAD_DIM), lambda b, h, s, qi, kj: (b * n + qi[s], h))
    kspec = pl.BlockSpec((t, g * HEAD_DIM), lambda b, h, s, qi, kj: (b * n + kj[s], h))
    return pl.pallas_call(
        functools.partial(_flash_kernel, scale=HEAD_DIM ** -0.5),
        out_shape=jax.ShapeDtypeStruct((batch * seq, MIX_W), BF16),
        grid_spec=pltpu.PrefetchScalarGridSpec(
            num_scalar_prefetch=2,
            grid=(batch, N_HEADS // g, len(pairs)),
            in_specs=[
                qspec, kspec, kspec,
                pl.BlockSpec((None, g, t, 1), lambda b, h, s, qi, kj: (b, h, qi[s], 0)),
                pl.BlockSpec((None, g, 1, t), lambda b, h, s, qi, kj: (b, h, 0, kj[s])),
            ],
            out_specs=qspec,
            scratch_shapes=[pltpu.VMEM((g, t, 1), F32), pltpu.VMEM((g, t, 1), F32),
                            pltpu.VMEM((g, t, HEAD_DIM), F32)],
        ),
        compiler_params=_params(("arbitrary", "arbitrary", "arbitrary")),
        name="fox_flash_attention",
    )(qi, kj, fq16, fk16, fv16, cum_col, cum_row)


def _suffix_kernel(pt_ref, x_ref, o_ref, xs_ref, *, n_pages):
    b = pl.program_id(0)
    for p in range(n_pages):
        xs_ref[p * N_HEADS:(p + 1) * N_HEADS, :] = x_ref[pt_ref[b * n_pages + p]]
    x = xs_ref[...]
    rows, t = x.shape
    later = jnp.where(_iota2((t, t), 0) > _iota2((t, t), 1), 1.0, 0.0).astype(BF16)
    r = _iota2((rows, rows), 0)
    col = _iota2((rows, rows), 1)
    later_pages = jnp.where(col // N_HEADS > r // N_HEADS,
                            jnp.where(col % N_HEADS == r % N_HEADS, 1.0, 0.0), 0.0).astype(BF16)
    totals = jnp.broadcast_to(jnp.sum(x, axis=1, keepdims=True), x.shape)
    o_ref[...] = _dot_exact_rhs(x, later) + _dot_exact_lhs(later_pages, totals)


def _suffix_sums(page_table_flat, logf_t, layer, n_batch, n_pages):
    n_pool = logf_t.shape[1]
    rows = n_pages * N_HEADS
    return pl.pallas_call(
        functools.partial(_suffix_kernel, n_pages=n_pages),
        out_shape=jax.ShapeDtypeStruct((n_batch, rows, PAGE_SIZE), F32),
        grid_spec=pltpu.PrefetchScalarGridSpec(
            num_scalar_prefetch=1,
            grid=(n_batch,),
            in_specs=[pl.BlockSpec((None, n_pool, N_HEADS, PAGE_SIZE), lambda b, pt: (layer, 0, 0, 0))],
            out_specs=pl.BlockSpec((None, rows, PAGE_SIZE), lambda b, pt: (b, 0, 0)),
            scratch_shapes=[pltpu.VMEM((rows, PAGE_SIZE), F32)],
        ),
        compiler_params=_params(("arbitrary",)),
        name="fox_past_suffix",
    )(page_table_flat, logf_t)


DECODE_PAGES = 8


def _decode_kernel(pt_ref, q_ref, *refs, scale, npg):
    k_refs, v_refs = refs[:npg], refs[npg:2 * npg]
    bias_ref, cq_ref, kn_ref, vn_ref, ckn_ref, o_ref, m_ref, l_ref, acc_ref = refs[2 * npg:]
    p = pl.program_id(1)

    @pl.when(p == 0)
    def _():
        m_ref[...] = jnp.full_like(m_ref, NEG_BIG)
        l_ref[...] = jnp.zeros_like(l_ref)
        acc_ref[...] = jnp.zeros_like(acc_ref)

    def head_rows(h):
        return slice(h * SUBLANES, (h + 1) * SUBLANES)

    def head_of_page(ref, h):
        return ref[pl.ds(h, PAGE_SIZE, stride=N_HEADS), :].astype(BF16)

    heads = range(N_HEADS)
    q16s = [q_ref[head_rows(h), :].astype(BF16) for h in heads]
    cqs = [cq_ref[head_rows(h), :] for h in heads]
    bias = bias_ref[...]
    ss = [jnp.concatenate(
        [_dot_nt(q16s[h], head_of_page(k_refs[i], h)) * scale
         + (cqs[h] + bias[i * N_HEADS + h:i * N_HEADS + h + 1, :]) for i in range(npg)], axis=1) for h in heads]
    m_prev = m_ref[...]
    m_new = jnp.maximum(m_prev, jnp.concatenate([jnp.max(s, axis=1, keepdims=True) for s in ss], axis=0))
    alpha = jnp.exp(m_prev - m_new)
    ps = [jnp.exp(ss[h] - m_new[head_rows(h), :]) for h in heads]
    l_ref[...] = alpha * l_ref[...] + jnp.concatenate([jnp.sum(p, axis=1, keepdims=True) for p in ps], axis=0)
    pv = jnp.concatenate(
        [_dot(ps[h].astype(BF16), jnp.concatenate([head_of_page(v_refs[i], h) for i in range(npg)], axis=0))
         for h in heads], axis=0)
    acc_ref[...] = alpha * acc_ref[...] + pv
    m_ref[...] = m_new

    @pl.when(p == pl.num_programs(1) - 1)
    def _():
        news = [slice(h * LANES, (h + 1) * LANES) for h in heads]
        ckn = ckn_ref[...]
        s2 = [_dot_nt(q16s[h], kn_ref[news[h], :]) * scale + (cqs[h] - ckn[h:h + 1, :]) for h in heads]
        keep = _iota2(s2[0].shape, 1) <= _iota2(s2[0].shape, 0)
        s2 = [jnp.where(keep, s, NEG_BIG) for s in s2]
        m_last = jnp.maximum(m_new, jnp.concatenate([jnp.max(s, axis=1, keepdims=True) for s in s2], axis=0))
        alpha2 = jnp.exp(m_new - m_last)
        p2 = [jnp.exp(s2[h] - m_last[head_rows(h), :]) for h in heads]
        l_last = alpha2 * l_ref[...] + jnp.concatenate([jnp.sum(x, axis=1, keepdims=True) for x in p2], axis=0)
        pv2 = jnp.concatenate([_dot(p2[h].astype(BF16), vn_ref[news[h], :]) for h in heads], axis=0)
        o_ref[...] = (alpha2 * acc_ref[...] + pv2) / l_last


def _decode_attention(page_table_flat, q64, cache_k, cache_v, bias, cq, kn, vn, ckn, layer, n_batch, n_pages):
    rows = N_HEADS * SUBLANES
    flat = PAGE_SIZE * N_HEADS
    npg = DECODE_PAGES
    page = lambda i: pl.BlockSpec((None, None, flat, HEAD_DIM),
                                  lambda b, p, pt: (layer, pt[b * n_pages + p * npg + i], 0, 0))
    per_b = lambda shape: pl.BlockSpec((None,) + shape, lambda b, p, pt: (b, 0, 0))
    return pl.pallas_call(
        functools.partial(_decode_kernel, scale=HEAD_DIM ** -0.5, npg=npg),
        out_shape=jax.ShapeDtypeStruct((n_batch, rows, HEAD_DIM), F32),
        grid_spec=pltpu.PrefetchScalarGridSpec(
            num_scalar_prefetch=1,
            grid=(n_batch, n_pages // npg),
            in_specs=[per_b((rows, HEAD_DIM))] + [page(i) for i in range(npg)] + [page(i) for i in range(npg)] + [
                pl.BlockSpec((None, npg * N_HEADS, PAGE_SIZE), lambda b, p, pt: (b, p, 0)),
                per_b((rows, 1)),
                per_b((N_HEADS * LANES, HEAD_DIM)), per_b((N_HEADS * LANES, HEAD_DIM)), per_b((N_HEADS, LANES)),
            ],
            out_specs=per_b((rows, HEAD_DIM)),
            scratch_shapes=[pltpu.VMEM((rows, 1), F32), pltpu.VMEM((rows, 1), F32),
                            pltpu.VMEM((rows, HEAD_DIM), F32)],
        ),
        compiler_params=_params(("arbitrary", "arbitrary")),
        name="fox_paged_decode",
    )(page_table_flat, q64, *([cache_k] * npg), *([cache_v] * npg), bias, cq, kn, vn, ckn)


def _unit_lower_inverses(ms, c):
    r = _iota2((c, c), 0)
    col = _iota2((c, c), 1)
    eye = jnp.where(r == col, 1.0, 0.0)
    pair = r // 2 == col // 2
    invs = [eye - jnp.where(pair, m, 0.0) for m in ms]
    s = 2
    while s < c:
        same_block = r // (2 * s) == col // (2 * s)
        other_half = r // s != col // s
        offs = [jnp.where(same_block, jnp.where(other_half, m, 0.0), 0.0).astype(BF16) for m in ms]
        inv16s = [inv.astype(BF16) for inv in invs]
        mids = [_dot(off, inv16).astype(BF16) for off, inv16 in zip(offs, inv16s)]
        invs = [inv - _dot(inv16, mid) for inv, inv16, mid in zip(invs, inv16s, mids)]
        s *= 2
    return invs


def _gdn_kernel(qkv_ref, convw_ref, conv0_ref, gates_ref, gates_t_ref, z_ref, nw_ref, s0_ref,
                o_ref, st_ref, ext_ref, s_ref, *, c, rows):
    n = pl.program_id(1)

    @pl.when(n == 0)
    def _():
        ext_ref[:, 0:SUBLANES, :] = conv0_ref[...]
        s_ref[...] = s0_ref[...]

    ext_ref[:, SUBLANES:SUBLANES + c, :] = qkv_ref[...]

    def conv(b, c0):
        acc = ext_ref[b, SUBLANES:SUBLANES + c, c0:c0 + HEAD_DIM] * convw_ref[CONV_W - 1:CONV_W, c0:c0 + HEAD_DIM]
        for i in range(1, CONV_W):
            acc = acc + (ext_ref[b, SUBLANES - i:SUBLANES - i + c, c0:c0 + HEAD_DIM]
                         * convw_ref[CONV_W - 1 - i:CONV_W - i, c0:c0 + HEAD_DIM])
        return _silu(acc)

    def l2n(x):
        return x * lax.rsqrt(jnp.sum(x * x, axis=-1, keepdims=True) + EPS)

    r = _iota2((c, c), 0)
    col = _iota2((c, c), 1)
    incl_lower = jnp.where(col <= r, 1.0, 0.0).astype(BF16)
    incl_upper = jnp.where(r <= col, 1.0, 0.0).astype(BF16)
    gates_b = [gates_ref[b] for b in range(rows)]
    gc_cols = [_dot_exact_lhs(incl_lower, g) for g in gates_b]
    gc_rows = [_dot_exact_rhs(gates_t_ref[b], incl_upper) for b in range(rows)]

    chains = [(b, h) for b in range(rows) for h in range(N_HEADS)]
    heads = range(len(chains))
    qs = [l2n(conv(b, h * HEAD_DIM)) * (HEAD_DIM ** -0.5) for b, h in chains]
    ks = [l2n(conv(b, MIX_W + h * HEAD_DIM)) for b, h in chains]
    vs = [conv(b, 2 * MIX_W + h * HEAD_DIM) for b, h in chains]
    ext_ref[:, 0:SUBLANES, :] = ext_ref[:, c:c + SUBLANES, :]
    states = [s_ref[b, h] for b, h in chains]
    betas = [gates_b[b][:, 16 + h:17 + h] for b, h in chains]
    gcs = [gc_cols[b][:, 8 + h:9 + h] for b, h in chains]
    gls = [gc[c - 1:c, :] for gc in gcs]
    decays = [jnp.exp(jnp.where(col <= r, gcs[i] - gc_rows[b][8 + h:9 + h, :], NEG_BIG))
              for i, (b, h) in enumerate(chains)]
    kbs = [ks[h] * betas[h] for h in heads]
    k16s = [k.astype(BF16) for k in ks]
    ms = [jnp.where(col < r, _dot_nt(kbs[h].astype(BF16), k16s[h]) * decays[h], 0.0) for h in heads]
    a16s = [(_dot_nt(qs[h].astype(BF16), k16s[h]) * decays[h]).astype(BF16) for h in heads]
    t16s = [t.astype(BF16) for t in _unit_lower_inverses(ms, c)]
    u0s = [_dot(t16s[h], (vs[h] * betas[h]).astype(BF16)) for h in heads]
    wk16s = [_dot(t16s[h], (kbs[h] * jnp.exp(gcs[h])).astype(BF16)).astype(BF16) for h in heads]
    s16s = [s.astype(BF16) for s in states]
    u16s = [(u0s[h] - _dot(wk16s[h], s16s[h])).astype(BF16) for h in heads]
    outs = [_dot((qs[h] * jnp.exp(gcs[h])).astype(BF16), s16s[h]) + _dot(a16s[h], u16s[h]) for h in heads]
    new_states = [states[h] * jnp.exp(gls[h]) + _dot_tn((ks[h] * jnp.exp(gls[h] - gcs[h])).astype(BF16), u16s[h])
                  for h in heads]
    for i, (b, h) in enumerate(chains):
        cols = slice(h * HEAD_DIM, (h + 1) * HEAD_DIM)
        s_ref[b, h] = new_states[i]
        o_ref[b, :, cols] = (_rms(outs[i]) * nw_ref[...] * _silu(z_ref[b, :, cols])).astype(o_ref.dtype)

    @pl.when(n == pl.num_programs(1) - 1)
    def _():
        st_ref[...] = s_ref[...]


GDN_ROWS = 2


def _gdn(qkv, conv_w, conv0, gates, gates_t, z, norm_w, s0, batch, n_chunks, c, out_dtype):
    m = qkv.shape[0]
    w3 = 3 * MIX_W
    seq = n_chunks * c
    rows = GDN_ROWS
    row = lambda width: pl.BlockSpec((rows, c, width), lambda b, n: (b, n, 0))
    state = pl.BlockSpec((rows, N_HEADS, HEAD_DIM, HEAD_DIM), lambda b, n: (b, 0, 0, 0))
    by_row = lambda a: a.reshape(batch, seq, a.shape[-1])
    o, st = pl.pallas_call(
        functools.partial(_gdn_kernel, c=c, rows=rows),
        out_shape=(jax.ShapeDtypeStruct((batch, seq, MIX_W), out_dtype),
                   jax.ShapeDtypeStruct((batch, N_HEADS, HEAD_DIM, HEAD_DIM), F32)),
        grid=(batch // rows, n_chunks),
        in_specs=[
            row(w3),
            pl.BlockSpec((CONV_W, w3), lambda b, n: (0, 0)),
            pl.BlockSpec((rows, SUBLANES, w3), lambda b, n: (b, 0, 0)),
            row(LANES),
            pl.BlockSpec((rows, None, LANES, c), lambda b, n: (b, n, 0, 0)),
            row(MIX_W),
            pl.BlockSpec((1, HEAD_DIM), lambda b, n: (0, 0)),
            state,
        ],
        out_specs=(row(MIX_W), state),
        scratch_shapes=[pltpu.VMEM((rows, c + SUBLANES, w3), F32),
                        pltpu.VMEM((rows, N_HEADS, HEAD_DIM, HEAD_DIM), F32)],
        compiler_params=_params(("arbitrary", "arbitrary")),
        name="gated_deltanet",
    )(by_row(qkv), conv_w, conv0, by_row(gates), gates_t.reshape(batch, n_chunks, LANES, c), by_row(z),
      norm_w.reshape(1, HEAD_DIM), s0)
    return o.reshape(m, MIX_W), st


def _ret_kernel(q_ref, k_ref, v_ref, g_ref, cos_ref, sin_ref, s0_ref, o_ref, st_ref, s_ref, *, c, n_valid):
    n = pl.program_id(1)

    @pl.when(n == 0)
    def _():
        s_ref[...] = s0_ref[...]

    cos = cos_ref[...]
    sin = sin_ref[...]

    def rot(x):
        return x * cos + pltpu.roll(x, HEAD_DIM // 2, 1) * sin

    r = _iota2((c, c), 0)
    col = _iota2((c, c), 1)
    cnt_r = jnp.minimum(r + 1, n_valid).astype(F32)
    cnt_c = jnp.minimum(col + 1, n_valid).astype(F32)
    cnt = cnt_r[:, 0:1]
    total = float(min(c, n_valid))

    chains = [(b, h) for b in range(s_ref.shape[0]) for h in range(N_HEADS)]
    heads = range(len(chains))
    lgs = [math.log(1.0 - 2.0 ** (-5.0 - h)) for _, h in chains]
    sls = [slice(h * HEAD_DIM, (h + 1) * HEAD_DIM) for _, h in chains]
    qs = [rot(q_ref[b, :, sls[i]]) for i, (b, _) in enumerate(chains)]
    ks = [rot(k_ref[b, :, sls[i]]) * (HEAD_DIM ** -0.5) for i, (b, _) in enumerate(chains)]
    v16s = [v_ref[b, :, sls[i]].astype(BF16) for i, (b, _) in enumerate(chains)]
    states = [s_ref[b, h] for b, h in chains]
    a16s = [(_dot_nt(qs[h].astype(BF16), ks[h].astype(BF16))
             * jnp.exp(jnp.where(col <= r, (cnt_r - cnt_c) * lgs[h], NEG_BIG))).astype(BF16) for h in heads]
    outs = [_dot(a16s[h], v16s[h]) + _dot((qs[h] * jnp.exp(cnt * lgs[h])).astype(BF16), states[h].astype(BF16))
            for h in heads]
    new_states = [states[h] * math.exp(total * lgs[h])
                  + _dot_tn((ks[h] * jnp.exp((total - cnt) * lgs[h])).astype(BF16), v16s[h]) for h in heads]
    for i, (b, h) in enumerate(chains):
        s_ref[b, h] = new_states[i]
        o_ref[b, :, sls[i]] = (_rms(outs[i]) * _silu(g_ref[b, :, sls[i]])).astype(o_ref.dtype)

    @pl.when(n == pl.num_programs(1) - 1)
    def _():
        st_ref[...] = s_ref[...]


def _retention(qkvg, cos, sin, s0, batch, n_chunks, c, n_valid, out_dtype):
    m = qkvg.shape[0]
    seq = n_chunks * c
    rows = GDN_ROWS
    colblk = lambda j: pl.BlockSpec((rows, c, MIX_W), lambda b, n: (b, n, j))
    table = pl.BlockSpec((c, HEAD_DIM), lambda b, n: (n, 0))
    state = pl.BlockSpec((rows, N_HEADS, HEAD_DIM, HEAD_DIM), lambda b, n: (b, 0, 0, 0))
    x = qkvg.reshape(batch, seq, 4 * MIX_W)
    o, st = pl.pallas_call(
        functools.partial(_ret_kernel, c=c, n_valid=n_valid),
        out_shape=(jax.ShapeDtypeStruct((batch, seq, MIX_W), out_dtype),
                   jax.ShapeDtypeStruct((batch, N_HEADS, HEAD_DIM, HEAD_DIM), F32)),
        grid=(batch // rows, n_chunks),
        in_specs=[colblk(0), colblk(1), colblk(2), colblk(3), table, table, state],
        out_specs=(colblk(0), state),
        scratch_shapes=[pltpu.VMEM((rows, N_HEADS, HEAD_DIM, HEAD_DIM), F32)],
        compiler_params=_params(("arbitrary", "arbitrary")),
        name="retention",
    )(x, x, x, x, cos, sin, s0)
    return o.reshape(m, MIX_W), st


def _merge_kernel(fox_ref, gdn_ref, ret_ref, w_ref, g0_ref, g1_ref, g2_ref, o_ref, w16_ref):
    @pl.when(pl.program_id(1) == 0)
    def _():
        w16_ref[...] = w_ref[...].astype(BF16)

    acc = _sigmoid(g0_ref[...]) * _dot(fox_ref[...], w16_ref[0])
    acc = acc + _sigmoid(g1_ref[...]) * _dot(gdn_ref[...], w16_ref[1])
    acc = acc + _sigmoid(g2_ref[...]) * _dot(ret_ref[...], w16_ref[2])
    o_ref[...] = acc.astype(o_ref.dtype)


def _merge(o_fox, o_gdn, o_ret, w_branch, layer, gate, tm, tn):
    m = o_fox.shape[0]
    d = w_branch.shape[3]
    nj = d // tn
    br = pl.BlockSpec((tm, MIX_W), lambda j, i: (i, 0))
    gspec = lambda b: pl.BlockSpec((tm, tn), lambda j, i: (i, b * nj + j))
    return pl.pallas_call(
        _merge_kernel,
        out_shape=jax.ShapeDtypeStruct((m, d), BF16),
        grid=(nj, m // tm),
        in_specs=[br, br, br,
                  pl.BlockSpec((None, N_BRANCH, MIX_W, tn), lambda j, i: (layer, 0, 0, j)),
                  gspec(0), gspec(1), gspec(2)],
        out_specs=pl.BlockSpec((tm, tn), lambda j, i: (i, j)),
        scratch_shapes=[pltpu.VMEM((N_BRANCH, MIX_W, tn), BF16)],
        compiler_params=_params(("arbitrary", "arbitrary")),
        name="branch_merge",
    )(o_fox, o_gdn, o_ret, w_branch, gate, gate, gate)


def _router_kernel(h_ref, whi_ref, wlo_ref, b_ref, idx_ref, wt_ref, cnt_ref, carry_ref):
    @pl.when(pl.program_id(0) == 0)
    def _():
        carry_ref[...] = jnp.zeros_like(carry_ref)

    h_hi, h_lo = _split2(h_ref[...])
    logits = _dot(h_hi, whi_ref[...]) + (_dot(h_hi, wlo_ref[...]) + _dot(h_lo, whi_ref[...])) + b_ref[...]
    tm = logits.shape[0]
    lane = _iota2(logits.shape, 1)
    lane_f = lane.astype(F32)
    logits = jnp.where(lane < N_EXPERTS, logits, -jnp.inf)
    vals, sels = [], []
    for k in range(TOP_K):
        top = jnp.max(logits, axis=1, keepdims=True)
        sel = jnp.min(jnp.where(logits == top, lane_f, float(LANES)), axis=1, keepdims=True)
        vals.append(top)
        sels.append(sel)
        logits = jnp.where(lane_f == sel, -jnp.inf, logits)
    chosen = [jnp.where(lane_f == sel, 1.0, 0.0) for sel in sels]
    picks = chosen[0] + chosen[1] + chosen[2] + chosen[3]
    earlier_rows = jnp.where(_iota2((tm, tm), 1) < _iota2((tm, tm), 0), 1.0, 0.0).astype(BF16)
    before = _dot(earlier_rows, picks.astype(BF16)) + carry_ref[0:1, :]
    carry_ref[0:1, :] = carry_ref[0:1, :] + jnp.sum(picks, axis=0, keepdims=True)
    es = [jnp.exp(v - vals[0]) for v in vals]
    tot = es[0] + es[1] + es[2] + es[3]
    idx = jnp.zeros(logits.shape, F32)
    wt = jnp.zeros(logits.shape, F32)
    for k in range(TOP_K):
        rank = jnp.sum(before * chosen[k], axis=1, keepdims=True)
        idx = jnp.where(lane == k, sels[k], jnp.where(lane == TOP_K + k, rank, idx))
        wt = jnp.where(lane == k, es[k] / tot, wt)
    idx_ref[...] = idx.astype(jnp.int32)
    wt_ref[...] = wt
    cnt_ref[...] = carry_ref[...]


def _router(h, m, w_hi, w_lo, b, tm):
    d = h.shape[1]
    tok = pl.BlockSpec((tm, LANES), lambda i: (i, 0))
    const = lambda rows: pl.BlockSpec((rows, LANES), lambda i: (0, 0))
    return pl.pallas_call(
        _router_kernel,
        out_shape=(jax.ShapeDtypeStruct((m, LANES), jnp.int32), jax.ShapeDtypeStruct((m, LANES), F32),
                   jax.ShapeDtypeStruct((SUBLANES, LANES), F32)),
        grid=(m // tm,),
        in_specs=[pl.BlockSpec((tm, d), lambda i: (i, 0)), const(d), const(d), const(1)],
        out_specs=(tok, tok, const(SUBLANES)),
        scratch_shapes=[pltpu.VMEM((SUBLANES, LANES), F32)],
        compiler_params=_params(("arbitrary",)),
        name="moe_router_topk",
    )(h, w_hi, w_lo, b)


def _row_copy(src_hbm, src_row, dst_vmem, dst_row, sem):
    return pltpu.make_async_copy(src_hbm.at[pl.ds(src_row, 1), :], dst_vmem.at[pl.ds(dst_row, 1), :], sem)


def _gather_kernel(tok_ref, nv_ref, h_ref, o_ref, buf_ref, sems):
    b = pl.program_id(0)
    blk = buf_ref.shape[1]
    slot = b % 2

    def fetch(block, to_slot):
        def issue(r, carry):
            _row_copy(h_ref, tok_ref[block * blk + r], buf_ref.at[to_slot], r, sems.at[to_slot]).start()
            return carry

        lax.fori_loop(0, blk, issue, 0, unroll=8)

    @pl.when(b == 0)
    def _():
        fetch(0, 0)

    @pl.when(b + 1 < nv_ref[0])
    def _():
        fetch(b + 1, 1 - slot)

    @pl.when(b < nv_ref[0])
    def _():
        pltpu.make_async_copy(h_ref.at[pl.ds(0, blk), :], buf_ref.at[slot], sems.at[slot]).wait()
        o_ref[...] = buf_ref[slot].astype(o_ref.dtype)

    @pl.when(b >= nv_ref[0])
    def _():
        o_ref[...] = jnp.zeros_like(o_ref)


def _gather_rows(slot_tok, n_valid, h_ext, n_blk, blk):
    d = h_ext.shape[1]
    return pl.pallas_call(
        _gather_kernel,
        out_shape=jax.ShapeDtypeStruct((n_blk * blk, d), BF16),
        grid_spec=pltpu.PrefetchScalarGridSpec(
            num_scalar_prefetch=2,
            grid=(n_blk,),
            in_specs=[pl.BlockSpec(memory_space=pl.ANY)],
            out_specs=pl.BlockSpec((blk, d), lambda b, tok, nv: (b, 0)),
            scratch_shapes=[pltpu.VMEM((2, blk, d), F32), pltpu.SemaphoreType.DMA((2,))],
        ),
        compiler_params=_params(("arbitrary",)),
        name="moe_token_gather",
    )(slot_tok, n_valid, h_ext)


def _stream_expert_weights(be_ref, first_ref, nxt_ref, copies, stage_ref, w16_ref):
    j = pl.program_id(0)
    b = pl.program_id(1)

    @pl.when(first_ref[b] == 1)
    def _():
        @pl.when(jnp.logical_and(j == 0, b == 0))
        def _():
            for c in copies(be_ref[0], 0):
                c.start()

        for c in copies(be_ref[b], j):
            c.wait()
        w16_ref[...] = stage_ref[...].astype(BF16)
        nxt = nxt_ref[b]

        @pl.when(nxt >= 0)
        def _():
            for c in copies(nxt, j):
                c.start()

        @pl.when(jnp.logical_and(nxt < 0, j + 1 < pl.num_programs(0)))
        def _():
            for c in copies(be_ref[0], j + 1):
                c.start()


def _gmm1_kernel(be_ref, nv_ref, first_ref, nxt_ref, x_ref, w_hbm, bg_ref, bl_ref, o_ref,
                 stage_ref, w16_ref, sems, *, layer, tn, d_ff):
    b = pl.program_id(1)

    def copies(e, sweep):
        return [pltpu.make_async_copy(
            w_hbm.at[layer, e, :, pl.ds(pl.multiple_of(half * d_ff + sweep * tn, tn), tn)],
            stage_ref.at[half], sems.at[half]) for half in range(2)]

    _stream_expert_weights(be_ref, first_ref, nxt_ref, copies, stage_ref, w16_ref)

    @pl.when(b < nv_ref[0])
    def _():
        x = x_ref[...]
        glu = jnp.minimum(_dot(x, w16_ref[0]) + bg_ref[...], SWIGLU_LIMIT)
        lin = jnp.clip(_dot(x, w16_ref[1]) + bl_ref[...], -SWIGLU_LIMIT, SWIGLU_LIMIT)
        o_ref[...] = (glu * _sigmoid(SWIGLU_ALPHA * glu) * (lin + 1.0)).astype(o_ref.dtype)

    @pl.when(b >= nv_ref[0])
    def _():
        o_ref[...] = jnp.zeros_like(o_ref)


def _gmm1(sched, xs, w_e1, b_e1, layer, n_blk, blk, tn):
    depth, n_exp, d, two_f = w_e1.shape
    d_ff = two_f // 2
    nj = d_ff // tn
    bspec = lambda off: pl.BlockSpec((None, None, 1, tn), lambda j, b, be, nv, fi, nx: (layer, be[b], 0, off + j))
    bias = b_e1.reshape(depth, n_exp, 1, two_f)
    return pl.pallas_call(
        functools.partial(_gmm1_kernel, layer=layer, tn=tn, d_ff=d_ff),
        out_shape=jax.ShapeDtypeStruct((n_blk * blk, d_ff), BF16),
        grid_spec=pltpu.PrefetchScalarGridSpec(
            num_scalar_prefetch=4,
            grid=(nj, n_blk),
            in_specs=[
                pl.BlockSpec((blk, d), lambda j, b, be, nv, fi, nx: (jnp.minimum(b, nv[0] - 1), 0)),
                pl.BlockSpec(memory_space=pl.ANY),
                bspec(0), bspec(nj),
            ],
            out_specs=pl.BlockSpec((blk, tn), lambda j, b, be, nv, fi, nx: (b, j)),
            scratch_shapes=[pltpu.VMEM((2, d, tn), F32), pltpu.VMEM((2, d, tn), BF16),
                            pltpu.SemaphoreType.DMA((2,))],
        ),
        compiler_params=_params(("arbitrary", "arbitrary")),
        name="moe_expert_up",
    )(*sched, xs, w_e1, bias, bias)


def _gmm2_kernel(be_ref, nv_ref, first_ref, nxt_ref, a_ref, w_hbm, b_ref, o_ref, stage_ref, w16_ref, sems,
                 *, layer, tn):
    b = pl.program_id(1)

    def copies(e, sweep):
        return [pltpu.make_async_copy(w_hbm.at[layer, e, :, pl.ds(pl.multiple_of(sweep * tn, tn), tn)],
                                      stage_ref, sems.at[0])]

    _stream_expert_weights(be_ref, first_ref, nxt_ref, copies, stage_ref, w16_ref)

    @pl.when(b < nv_ref[0])
    def _():
        o_ref[...] = _dot(a_ref[...], w16_ref[...]) + b_ref[...]

    @pl.when(b >= nv_ref[0])
    def _():
        o_ref[...] = jnp.zeros_like(o_ref)


def _gmm2(sched, act, w_e2, b_e2, layer, n_blk, blk, tn):
    depth, n_exp, d_ff, d = w_e2.shape
    return pl.pallas_call(
        functools.partial(_gmm2_kernel, layer=layer, tn=tn),
        out_shape=jax.ShapeDtypeStruct((n_blk * blk, d), F32),
        grid_spec=pltpu.PrefetchScalarGridSpec(
            num_scalar_prefetch=4,
            grid=(d // tn, n_blk),
            in_specs=[
                pl.BlockSpec((blk, d_ff), lambda j, b, be, nv, fi, nx: (jnp.minimum(b, nv[0] - 1), 0)),
                pl.BlockSpec(memory_space=pl.ANY),
                pl.BlockSpec((None, None, 1, tn), lambda j, b, be, nv, fi, nx: (layer, be[b], 0, j)),
            ],
            out_specs=pl.BlockSpec((blk, tn), lambda j, b, be, nv, fi, nx: (b, j)),
            scratch_shapes=[pltpu.VMEM((d_ff, tn), F32), pltpu.VMEM((d_ff, tn), BF16),
                            pltpu.SemaphoreType.DMA((1,))],
        ),
        compiler_params=_params(("arbitrary", "arbitrary")),
        name="moe_expert_down",
    )(*sched, act, w_e2, b_e2.reshape(depth, n_exp, 1, d))


def _combine_kernel(pos_ref, ys_ref, wt_ref, x_ref, gate_ref, o_ref, buf_ref, sems):
    tm = x_ref.shape[0]
    i = pl.program_id(0)
    slot = i % 2

    def fetch(tile, to_slot):
        base = tile * tm * TOP_K

        def issue(t, carry):
            for k in range(TOP_K):
                _row_copy(ys_ref, pos_ref[base + t * TOP_K + k], buf_ref.at[to_slot, k], t, sems.at[to_slot]).start()
            return carry

        lax.fori_loop(0, tm, issue, 0, unroll=2)

    @pl.when(i == 0)
    def _():
        fetch(0, 0)

    @pl.when(i + 1 < pl.num_programs(0))
    def _():
        fetch(i + 1, 1 - slot)

    for k in range(TOP_K):
        pltpu.make_async_copy(ys_ref.at[pl.ds(0, tm), :], buf_ref.at[slot, k], sems.at[slot]).wait()
    wt = wt_ref[...]
    acc = buf_ref[slot, 0] * wt[:, 0:1]
    for k in range(1, TOP_K):
        acc = acc + buf_ref[slot, k] * wt[:, k:k + 1]
    o_ref[...] = x_ref[...] + gate_ref[...] * acc


def _combine(pos, ys, wt, x, gate, tm, rows_per_group):
    m, d = x.shape
    r = gate.shape[1]
    tiles = rows_per_group // tm
    return pl.pallas_call(
        _combine_kernel,
        out_shape=jax.ShapeDtypeStruct((m, d), F32),
        grid_spec=pltpu.PrefetchScalarGridSpec(
            num_scalar_prefetch=1,
            grid=(m // tm,),
            in_specs=[
                pl.BlockSpec(memory_space=pl.ANY),
                pl.BlockSpec((tm, LANES), lambda i, pos: (i, 0)),
                pl.BlockSpec((tm, d), lambda i, pos: (i, 0)),
                pl.BlockSpec((None, r, d), lambda i, pos: (i // tiles, 0, 0)),
            ],
            out_specs=pl.BlockSpec((tm, d), lambda i, pos: (i, 0)),
            scratch_shapes=[pltpu.VMEM((2, TOP_K, tm, d), F32), pltpu.SemaphoreType.DMA((2,))],
        ),
        compiler_params=_params(("arbitrary",)),
        name="moe_combine",
    )(pos, ys, wt, x, gate)


TM = 512
TN = 1024


def _pad_lanes(a):
    return jnp.pad(a, ((0, 0), (0, LANES - a.shape[1])))


def _layer_weights(l, d, w_in, fox_f_bias, gdn_a_log, gdn_dt_bias, w_router, b_router):
    w = w_in[l]
    o_f = 3 * MIX_W
    o_gqkv = o_f + N_HEADS
    o_ga = o_gqkv + 3 * MIX_W
    o_gb = o_ga + N_HEADS
    o_gz = o_gb + N_HEADS
    o_ret = o_gz + MIX_W
    o_gate = o_ret + 4 * MIX_W
    small = _pad_lanes(jnp.concatenate([w[:, o_f:o_gqkv], w[:, o_ga:o_gb], w[:, o_gb:o_gz]], axis=1))
    ws_hi, ws_lo = _split2(small)
    prm = jnp.zeros((SUBLANES, LANES), F32)
    prm = prm.at[0, 0:8].set(fox_f_bias[l]).at[0, 8:16].set(gdn_dt_bias[l])
    prm = prm.at[1, 8:16].set(-jnp.exp(gdn_a_log[l]))
    wr_hi, wr_lo = _split2(_pad_lanes(w_router[l]))
    return dict(
        w_fox=w[:, 0:o_f].astype(BF16),
        w_gqkv=w[:, o_gqkv:o_ga].astype(BF16),
        w_gz=w[:, o_gz:o_ret].astype(BF16),
        w_ret=w[:, o_ret:o_gate].astype(BF16),
        w_gate=w[:, o_gate:o_gate + N_BRANCH * d].astype(BF16),
        ws_hi=ws_hi, ws_lo=ws_lo, prm=prm, wr_hi=wr_hi, wr_lo=wr_lo,
        b_router=_pad_lanes(b_router[l].reshape(1, N_EXPERTS)),
    )


def _rope_tables(pos):
    half = HEAD_DIM // 2
    inv = ROPE_BASE ** (-jnp.arange(half, dtype=F32) / half)
    ang = pos.astype(F32)[:, None] * inv[None, :]
    cos, sin = jnp.cos(ang), jnp.sin(ang)
    return jnp.concatenate([cos, cos], axis=1), jnp.concatenate([-sin, sin], axis=1)


def _project(h, lw, tm):
    fox = _mm(h, lw['w_fox'], tm, TN)
    gqkv = _mm(h, lw['w_gqkv'], tm, TN)
    gz = _mm(h, lw['w_gz'], tm, TN)
    ret = _mm(h, lw['w_ret'], tm, TN)
    gate = _mm(h, lw['w_gate'], tm, TN)
    gates = _gates(h, lw['ws_hi'], lw['ws_lo'], lw['prm'], tm)
    return fox, gqkv, gz, ret, gate, gates


def _prompt_mixer(x, mod, lw, p, batch, seq):
    m, d = x.shape
    sh1, sc1, g1, sh2, sc2, g2 = mod
    h = _norm_mod(x, p['norm1'], sc1, sh1, TM, seq)
    fox, gqkv, gz, ret, gate, gates = _project(h, lw, TM)
    cum = _cumsum_rows(gates, 256, seq)
    fq16, fk, fk16, fv16 = _fox_prep(fox, p['fox_qn'], p['fox_kn'], TM)
    cum_t = cum[:, :N_HEADS].reshape(batch, seq, N_HEADS).transpose(0, 2, 1)
    o_fox = _flash_attention(fq16, fk16, fv16, cum_t[..., None], cum_t[:, :, None, :], batch, seq, 512)
    n_chunks = seq // CHUNK
    gates_t = gates.reshape(m // CHUNK, CHUNK, LANES).transpose(0, 2, 1)
    zero_state = jnp.zeros((batch, N_HEADS, HEAD_DIM, HEAD_DIM), F32)
    o_gdn, s_gdn = _gdn(gqkv, p['gdn_conv'], jnp.zeros((batch, SUBLANES, 3 * MIX_W), F32), gates, gates_t, gz,
                        p['gdn_nw'], zero_state, batch, n_chunks, CHUNK, BF16)
    cos, sin = _rope_tables(jnp.arange(seq, dtype=jnp.int32))
    o_ret, s_ret = _retention(ret, cos, sin, zero_state, batch, n_chunks, CHUNK, CHUNK, BF16)
    merged = _merge(o_fox, o_gdn, o_ret, p['w_branch'], p['layer'], gate, TM, 512)
    x1 = _mm_residual(merged, p['w_out'], p['layer'], x, g1, TM, TN, seq)
    state = (fk.reshape(batch, seq, N_HEADS, HEAD_DIM),
             fox[:, 2 * MIX_W:].reshape(batch, seq, N_HEADS, HEAD_DIM),
             gates[:, :N_HEADS].reshape(batch, seq, N_HEADS),
             s_gdn,
             gqkv.reshape(batch, seq, 3 * MIX_W)[:, seq - (CONV_W - 1):],
             s_ret)
    return x1, state


def _sample_mixer(x, mod, lw, p, past, layer, batch, seq, past_len):
    m, d = x.shape
    sh1, sc1, g1, sh2, sc2, g2 = mod
    n_pages = past_len // PAGE_SIZE
    h = _norm_mod(x, p['norm1'], sc1, sh1, m, m)
    fox, gqkv, gz, ret, gate, gates = _project(h, lw, m)
    cum = _cumsum_rows(gates, m, seq)
    fq16, fk, fk16, fv16 = _fox_prep(fox, p['fox_qn'], p['fox_kn'], m)

    pad_q = SUBLANES - seq
    def head_major(a, pad_to):
        a = a.reshape(batch, seq, N_HEADS, HEAD_DIM).transpose(0, 2, 1, 3)
        return jnp.pad(a, ((0, 0), (0, 0), (0, pad_to - seq), (0, 0)))

    q64 = head_major(fq16, SUBLANES).reshape(batch, N_HEADS * SUBLANES, HEAD_DIM).astype(F32)
    cum_t = cum[:, :N_HEADS].reshape(batch, seq, N_HEADS).transpose(0, 2, 1)
    cq = jnp.pad(cum_t, ((0, 0), (0, 0), (0, pad_q))).reshape(batch, N_HEADS * SUBLANES, 1)
    kn = head_major(fk16, LANES).reshape(batch, N_HEADS * LANES, HEAD_DIM)
    vn = head_major(fv16, LANES).reshape(batch, N_HEADS * LANES, HEAD_DIM)
    ckn = jnp.pad(cum_t, ((0, 0), (0, 0), (0, LANES - seq)))
    bias = _suffix_sums(past['pt'], past['logf_t'], layer, batch, n_pages)
    o = _decode_attention(past['pt'], q64, past['k'], past['v'], bias, cq, kn, vn, ckn, layer, batch, n_pages)
    o_fox = (o.reshape(batch, N_HEADS, SUBLANES, HEAD_DIM)[:, :, :seq].transpose(0, 2, 1, 3)
             .reshape(m, MIX_W).astype(BF16))

    def pad_rows(a):
        return jnp.pad(a.reshape(batch, seq, -1), ((0, 0), (0, pad_q), (0, 0))).reshape(batch * SUBLANES, -1)

    def unpad_rows(a):
        return a.reshape(batch, SUBLANES, -1)[:, :seq].reshape(m, -1)

    gates_p = pad_rows(gates)
    gates_t = gates_p.reshape(batch, SUBLANES, LANES).transpose(0, 2, 1)
    conv0 = jnp.pad(past['conv'], ((0, 0), (SUBLANES - (CONV_W - 1), 0), (0, 0)))
    o_gdn, s_gdn = _gdn(pad_rows(gqkv), p['gdn_conv'], conv0, gates_p, gates_t, pad_rows(gz), p['gdn_nw'],
                        past['gdn'], batch, 1, SUBLANES, F32)
    cos, sin = _rope_tables(past_len + jnp.arange(seq, dtype=jnp.int32))
    cos = jnp.pad(cos, ((0, pad_q), (0, 0)))
    sin = jnp.pad(sin, ((0, pad_q), (0, 0)))
    o_ret, s_ret = _retention(pad_rows(ret), cos, sin, past['ret'], batch, 1, SUBLANES, seq, F32)
    o_gdn = unpad_rows(o_gdn).astype(BF16)
    o_ret = unpad_rows(o_ret).astype(BF16)

    merged = _merge(o_fox, o_gdn, o_ret, p['w_branch'], p['layer'], gate, m, 512)
    x1 = _mm_residual(merged, p['w_out'], p['layer'], x, g1, m, TN, m)
    state = (fk.reshape(batch, seq, N_HEADS, HEAD_DIM),
             fox[:, 2 * MIX_W:].reshape(batch, seq, N_HEADS, HEAD_DIM),
             gates[:, :N_HEADS].reshape(batch, seq, N_HEADS),
             s_gdn,
             gqkv.reshape(batch, seq, 3 * MIX_W)[:, seq - (CONV_W - 1):],
             s_ret)
    return x1, state


def _moe(h_ext, n, lw, w_e1, b_e1, w_e2, b_e2, layer):
    idx, wt, cnt = _router(h_ext, n, lw['wr_hi'], lw['wr_lo'], lw['b_router'], LANES)
    flat_e = idx[:, :TOP_K].reshape(-1)
    rank = idx[:, TOP_K:2 * TOP_K].reshape(-1)
    counts = cnt[0, :N_EXPERTS].astype(jnp.int32)
    n_assign = n * TOP_K
    padded = (counts + MOE_BLK - 1) // MOE_BLK * MOE_BLK
    pad_end = jnp.cumsum(padded)
    dest = ((pad_end - padded)[flat_e] + rank).astype(jnp.int32)
    n_blk = -(-n_assign // MOE_BLK) + N_EXPERTS
    slot_tok = jnp.full((n_blk * MOE_BLK,), n, jnp.int32).at[dest].set(
        jnp.arange(n_assign, dtype=jnp.int32) // TOP_K, unique_indices=True)
    blk_ids = jnp.arange(n_blk, dtype=jnp.int32)
    blk_e = jnp.sum((blk_ids[:, None] * MOE_BLK >= pad_end[None, :]).astype(jnp.int32), axis=1)
    n_valid = (pad_end[-1] // MOE_BLK).astype(jnp.int32)
    blk_e = jnp.where(blk_ids < n_valid, jnp.minimum(blk_e, N_EXPERTS - 1), 0)
    blk_e = jnp.where(blk_ids < n_valid, blk_e, jnp.max(blk_e)).astype(jnp.int32)
    prev_e = jnp.concatenate([jnp.full((1,), -1, jnp.int32), blk_e[:-1]])
    first = jnp.where(blk_ids < n_valid, blk_e != prev_e, False).astype(jnp.int32)
    experts = jnp.arange(N_EXPERTS, dtype=jnp.int32)
    later = jnp.where((experts[None, :] > blk_e[:, None]) & (counts[None, :] > 0), experts[None, :], N_EXPERTS)
    nxt = jnp.min(later, axis=1)
    nxt = jnp.where(nxt < N_EXPERTS, nxt, -1).astype(jnp.int32)
    n_valid = n_valid.reshape(1)
    sched = (blk_e, n_valid, first, nxt)
    xs = _gather_rows(slot_tok, n_valid, h_ext, n_blk, MOE_BLK)
    tn = min(MOE_TN, w_e2.shape[2], w_e2.shape[3])
    act = _gmm1(sched, xs, w_e1, b_e1, layer, n_blk, MOE_BLK, tn)
    ys = _gmm2(sched, act, w_e2, b_e2, layer, n_blk, MOE_BLK, min(2 * MOE_TN, w_e2.shape[3]))
    return ys, dest, wt


def kernel(x_prompt, x_sample, cache_fox_k, cache_fox_v, cache_fox_logf, state_gdn, state_gdn_conv, state_ret,
           page_table, c_prompt, c_sample, norm1_w, norm2_w, w_ada, b_ada, w_in, fox_f_bias, fox_qn_w, fox_kn_w,
           gdn_conv_w, gdn_a_log, gdn_dt_bias, gdn_norm_w, w_branch, w_out, w_router, b_router,
           w_e1, b_e1, w_e2, b_e2):
    batch, seq, d = x_prompt.shape
    dec_batch, dec_seq, _ = x_sample.shape
    depth = w_in.shape[0]
    n_pool = cache_fox_k.shape[1]
    n_pages = page_table.shape[1]
    past_len = n_pages * PAGE_SIZE
    mp = batch * seq
    ms = dec_batch * dec_seq

    n_cond = batch + dec_batch
    c_all = jnp.concatenate([c_prompt, c_sample, jnp.zeros((-n_cond % SUBLANES, d), F32)], axis=0)
    mod_all = _modulation(c_all, w_ada, b_ada)

    cache_k = cache_fox_k.reshape(depth, n_pool, PAGE_SIZE * N_HEADS, HEAD_DIM)
    cache_v = cache_fox_v.reshape(depth, n_pool, PAGE_SIZE * N_HEADS, HEAD_DIM)
    logf_t = cache_fox_logf.transpose(0, 1, 3, 2)
    pt_flat = page_table.reshape(-1).astype(jnp.int32)

    xp = x_prompt.reshape(mp, d)
    xs = x_sample.reshape(ms, d)
    st_p, st_s = [], []
    for l in range(depth):
        lw = _layer_weights(l, d, w_in, fox_f_bias, gdn_a_log, gdn_dt_bias, w_router, b_router)
        p = dict(norm1=norm1_w[l], norm2=norm2_w[l], fox_qn=fox_qn_w[l], fox_kn=fox_kn_w[l],
                 gdn_conv=gdn_conv_w[l], gdn_nw=gdn_norm_w[l], w_branch=w_branch, w_out=w_out, layer=l)
        mod_p = [t[:, None, :] for t in jnp.split(mod_all[l, :batch], 6, axis=-1)]
        mod_s = [t[None] for t in jnp.split(jnp.repeat(mod_all[l, batch:n_cond], dec_seq, axis=0), 6, axis=-1)]
        past = dict(pt=pt_flat, k=cache_k, v=cache_v, logf_t=logf_t, gdn=state_gdn[l],
                    conv=state_gdn_conv[l], ret=state_ret[l])
        x1p, sp = _prompt_mixer(xp, mod_p, lw, p, batch, seq)
        x1s, ss = _sample_mixer(xs, mod_s, lw, p, past, l, dec_batch, dec_seq, past_len)
        h_ext = jnp.zeros((mp + ms + SUBLANES, d), F32)
        h_ext = _norm_mod_into(h_ext, x1p, p['norm2'], mod_p[4], mod_p[3], TM, seq, 0)
        h_ext = _norm_mod_into(h_ext, x1s, p['norm2'], mod_s[4], mod_s[3], ms, ms, mp // ms)
        ys, dest, wt = _moe(h_ext, mp + ms, lw, w_e1, b_e1, w_e2, b_e2, l)
        xp = _combine(dest[:mp * TOP_K], ys, wt[:mp], x1p, mod_p[5], LANES, seq)
        xs = _combine(dest[mp * TOP_K:], ys, wt[mp:], x1s, mod_s[5], LANES, ms)
        st_p.append(sp)
        st_s.append(ss)

    def stk(sts, i):
        return jnp.stack([s[i] for s in sts])

    return (xp.reshape(batch, seq, d), xs.reshape(dec_batch, dec_seq, d),
            stk(st_p, 0), stk(st_p, 1), stk(st_p, 2), stk(st_p, 3), stk(st_p, 4), stk(st_p, 5),
            stk(st_s, 0), stk(st_s, 1), stk(st_s, 2), stk(st_s, 3), stk(st_s, 4), stk(st_s, 5))
```

```python
import functools
import math

import jax
import jax.numpy as jnp
from jax import lax
from jax.experimental import pallas as pl
from jax.experimental.pallas import tpu as pltpu

F32 = jnp.float32
BF16 = jnp.bfloat16

HEAD_DIM = 128
N_HEADS = 8
MIX_W = N_HEADS * HEAD_DIM
N_BRANCH = 3
CONV_W = 4
CHUNK = 64
PAGE_SIZE = 128
ROPE_BASE = 10000.0
N_EXPERTS = 32
TOP_K = 4
SWIGLU_LIMIT = 7.0
SWIGLU_ALPHA = 1.702
EPS = 1e-6

LANES = 128
SUBLANES = 8
VMEM_LIMIT = 48 * 1024 * 1024
MOE_VMEM_LIMIT = 56 * 1024 * 1024
MOE_BLK = 256
MOE_TN = 1024
NEG_BIG = -1e30


def _params(sem, vmem=VMEM_LIMIT):
    return pltpu.CompilerParams(dimension_semantics=sem, vmem_limit_bytes=vmem)


def _dot(a, b):
    return jnp.dot(a, b, preferred_element_type=F32)


def _dot_nt(a, b):
    return lax.dot_general(a, b, (((1,), (1,)), ((), ())), preferred_element_type=F32)


def _dot_tn(a, b):
    return lax.dot_general(a, b, (((0,), (0,)), ((), ())), preferred_element_type=F32)


def _split2(a):
    hi = a.astype(BF16)
    lo = (a - hi.astype(F32)).astype(BF16)
    return hi, lo


def _split3(a):
    hi = a.astype(BF16)
    r = a - hi.astype(F32)
    mid = r.astype(BF16)
    lo = (r - mid.astype(F32)).astype(BF16)
    return hi, mid, lo


def _dot_exact_lhs(mask_bf16, x):
    hi, mid, lo = _split3(x)
    return _dot(mask_bf16, hi) + (_dot(mask_bf16, mid) + _dot(mask_bf16, lo))


def _dot_exact_rhs(x, mask_bf16):
    hi, mid, lo = _split3(x)
    return _dot(hi, mask_bf16) + (_dot(mid, mask_bf16) + _dot(lo, mask_bf16))


def _sigmoid(x):
    return 1.0 / (1.0 + jnp.exp(-x))


def _silu(x):
    return x * _sigmoid(x)


def _softplus(x):
    return jnp.maximum(x, 0.0) + jnp.log1p(jnp.exp(-jnp.abs(x)))


def _iota2(shape, dim):
    return lax.broadcasted_iota(jnp.int32, shape, dim)


def _modulation_kernel(c_ref, w_ref, b_ref, o_ref):
    c = _silu(c_ref[...])
    c_hi, c_lo = _split2(c)
    w16 = w_ref[...].astype(BF16)
    o_ref[...] = _dot(c_hi, w16) + _dot(c_lo, w16) + b_ref[...]


def _modulation(c_all, w_ada, b_ada, tn=1024):
    depth, d, n = w_ada.shape
    rows = c_all.shape[0]
    return pl.pallas_call(
        _modulation_kernel,
        out_shape=jax.ShapeDtypeStruct((depth, rows, n), F32),
        grid=(depth, n // tn),
        in_specs=[
            pl.BlockSpec((rows, d), lambda l, j: (0, 0)),
            pl.BlockSpec((None, d, tn), lambda l, j: (l, 0, j)),
            pl.BlockSpec((None, 1, tn), lambda l, j: (l, 0, j)),
        ],
        out_specs=pl.BlockSpec((None, rows, tn), lambda l, j: (l, 0, j)),
        compiler_params=_params(("arbitrary", "arbitrary")),
        name="adaln_modulation",
    )(c_all, w_ada, b_ada.reshape(depth, 1, n))


def _rms(x):
    return x * lax.rsqrt(jnp.mean(x * x, axis=-1, keepdims=True) + EPS)


def _norm_mod_kernel(x_ref, w_ref, sc_ref, sh_ref, o_ref):
    y = _rms(x_ref[...]) * w_ref[...]
    o_ref[...] = (y * (1.0 + sc_ref[...]) + sh_ref[...]).astype(o_ref.dtype)


def _mod_spec(mod, tm, rows_per_group):
    r, d = mod.shape[1], mod.shape[2]
    tiles = rows_per_group // tm
    return pl.BlockSpec((None, r, d), lambda i: (i // tiles, 0, 0))


def _norm_mod(x, w, sc, sh, tm, rows_per_group, out_dtype=BF16):
    m, d = x.shape
    return pl.pallas_call(
        _norm_mod_kernel,
        out_shape=jax.ShapeDtypeStruct((m, d), out_dtype),
        grid=(m // tm,),
        in_specs=[
            pl.BlockSpec((tm, d), lambda i: (i, 0)),
            pl.BlockSpec((1, d), lambda i: (0, 0)),
            _mod_spec(sc, tm, rows_per_group),
            _mod_spec(sh, tm, rows_per_group),
        ],
        out_specs=pl.BlockSpec((tm, d), lambda i: (i, 0)),
        compiler_params=_params(("arbitrary",)),
        name="norm_modulate",
    )(x, w.reshape(1, d), sc, sh)


def _norm_mod_into_kernel(buf_ref, x_ref, w_ref, sc_ref, sh_ref, o_ref):
    del buf_ref
    _norm_mod_kernel(x_ref, w_ref, sc_ref, sh_ref, o_ref)


def _norm_mod_into(buf, x, w, sc, sh, tm, rows_per_group, first_block):
    m, d = x.shape
    return pl.pallas_call(
        _norm_mod_into_kernel,
        out_shape=jax.ShapeDtypeStruct(buf.shape, buf.dtype),
        grid=(m // tm,),
        in_specs=[
            pl.BlockSpec(memory_space=pl.ANY),
            pl.BlockSpec((tm, d), lambda i: (i, 0)),
            pl.BlockSpec((1, d), lambda i: (0, 0)),
            _mod_spec(sc, tm, rows_per_group),
            _mod_spec(sh, tm, rows_per_group),
        ],
        out_specs=pl.BlockSpec((tm, d), lambda i: (first_block + i, 0)),
        input_output_aliases={0: 0},
        compiler_params=_params(("arbitrary",)),
        name="norm_modulate_into",
    )(buf, x, w.reshape(1, d), sc, sh)


def _mm_kernel(a_ref, w_ref, o_ref, w16_ref):
    @pl.when(pl.program_id(1) == 0)
    def _():
        w16_ref[...] = w_ref[...].astype(BF16)

    o_ref[...] = _dot(a_ref[...], w16_ref[...]).astype(o_ref.dtype)


def _mm_residual_kernel(a_ref, w_ref, res_ref, gate_ref, o_ref, w16_ref):
    @pl.when(pl.program_id(1) == 0)
    def _():
        w16_ref[...] = w_ref[...].astype(BF16)

    o_ref[...] = res_ref[...] + gate_ref[...] * _dot(a_ref[...], w16_ref[...])


def _mm(a, w, tm, tn, out_dtype=F32):
    m, k = a.shape
    n = w.shape[1]
    return pl.pallas_call(
        _mm_kernel,
        out_shape=jax.ShapeDtypeStruct((m, n), out_dtype),
        grid=(n // tn, m // tm),
        in_specs=[
            pl.BlockSpec((tm, k), lambda j, i: (i, 0)),
            pl.BlockSpec((k, tn), lambda j, i: (0, j)),
        ],
        out_specs=pl.BlockSpec((tm, tn), lambda j, i: (i, j)),
        scratch_shapes=[pltpu.VMEM((k, tn), BF16)],
        compiler_params=_params(("arbitrary", "arbitrary")),
        name="matmul",
    )(a, w)


def _mm_residual(a, w, layer, res, gate, tm, tn, rows_per_group):
    m, k = a.shape
    n = w.shape[2]
    r = gate.shape[1]
    tiles = rows_per_group // tm
    return pl.pallas_call(
        _mm_residual_kernel,
        out_shape=jax.ShapeDtypeStruct((m, n), F32),
        grid=(n // tn, m // tm),
        in_specs=[
            pl.BlockSpec((tm, k), lambda j, i: (i, 0)),
            pl.BlockSpec((None, k, tn), lambda j, i: (layer, 0, j)),
            pl.BlockSpec((tm, tn), lambda j, i: (i, j)),
            pl.BlockSpec((None, r, tn), lambda j, i: (i // tiles, 0, j)),
        ],
        out_specs=pl.BlockSpec((tm, tn), lambda j, i: (i, j)),
        scratch_shapes=[pltpu.VMEM((k, tn), BF16)],
        compiler_params=_params(("arbitrary", "arbitrary")),
        name="matmul_residual",
    )(a, w, res, gate)


def _gates_kernel(h_ref, whi_ref, wlo_ref, prm_ref, o_ref):
    h = h_ref[...]
    z = _dot(h, whi_ref[...]) + _dot(h, wlo_ref[...]) + prm_ref[0:1, :]
    lane = _iota2(z.shape, 1)
    tail = jnp.log1p(jnp.exp(-jnp.abs(z)))
    lf = jnp.minimum(z, 0.0) - tail
    g = prm_ref[1:2, :] * (jnp.maximum(z, 0.0) + tail)
    beta = _sigmoid(z)
    o_ref[...] = jnp.where(lane < 8, lf, jnp.where(lane < 16, g, beta))


def _gates(h, w_hi, w_lo, prm, tm):
    m, d = h.shape
    return pl.pallas_call(
        _gates_kernel,
        out_shape=jax.ShapeDtypeStruct((m, LANES), F32),
        grid=(m // tm,),
        in_specs=[
            pl.BlockSpec((tm, d), lambda i: (i, 0)),
            pl.BlockSpec((d, LANES), lambda i: (0, 0)),
            pl.BlockSpec((d, LANES), lambda i: (0, 0)),
            pl.BlockSpec((SUBLANES, LANES), lambda i: (0, 0)),
        ],
        out_specs=pl.BlockSpec((tm, LANES), lambda i: (i, 0)),
        compiler_params=_params(("arbitrary",)),
        name="gate_projection",
    )(h, w_hi, w_lo, prm)


def _cumsum_kernel(x_ref, o_ref, carry_ref, *, group):
    tm = x_ref.shape[0]
    r = _iota2((tm, tm), 0)
    c = _iota2((tm, tm), 1)
    if group >= tm:
        tri = jnp.where(c <= r, 1.0, 0.0).astype(BF16)
    else:
        tri = jnp.where(c <= r, jnp.where(r // group == c // group, 1.0, 0.0), 0.0).astype(BF16)
    cs = _dot_exact_lhs(tri, x_ref[...])
    if group > tm:
        @pl.when(pl.program_id(0) % (group // tm) == 0)
        def _():
            carry_ref[...] = jnp.zeros_like(carry_ref)

        cs = cs + carry_ref[...]
        carry_ref[...] = cs[tm - 1:tm, :]
    o_ref[...] = cs


def _cumsum_rows(x, tm, group):
    m, n = x.shape
    return pl.pallas_call(
        functools.partial(_cumsum_kernel, group=group),
        out_shape=jax.ShapeDtypeStruct((m, n), F32),
        grid=(m // tm,),
        in_specs=[pl.BlockSpec((tm, n), lambda i: (i, 0))],
        out_specs=pl.BlockSpec((tm, n), lambda i: (i, 0)),
        scratch_shapes=[pltpu.VMEM((1, n), F32)],
        compiler_params=_params(("arbitrary",)),
        name="cumsum_rows",
    )(x)


def _fox_prep_kernel(q_ref, k_ref, v_ref, qw_ref, kw_ref, fq16_ref, fk_ref, fk16_ref, fv16_ref):
    for h in range(N_HEADS):
        sl = slice(h * HEAD_DIM, (h + 1) * HEAD_DIM)
        fq16_ref[:, sl] = (_rms(q_ref[:, sl]) * qw_ref[...]).astype(BF16)
        kn = _rms(k_ref[:, sl]) * kw_ref[...]
        fk_ref[:, sl] = kn
        fk16_ref[:, sl] = kn.astype(BF16)
    fv16_ref[...] = v_ref[...].astype(BF16)


def _fox_prep(qkv, qw, kw, tm):
    m = qkv.shape[0]
    col = lambda c: pl.BlockSpec((tm, MIX_W), lambda i: (i, c))
    wspec = pl.BlockSpec((1, HEAD_DIM), lambda i: (0, 0))
    return pl.pallas_call(
        _fox_prep_kernel,
        out_shape=(
            jax.ShapeDtypeStruct((m, MIX_W), BF16),
            jax.ShapeDtypeStruct((m, MIX_W), F32),
            jax.ShapeDtypeStruct((m, MIX_W), BF16),
            jax.ShapeDtypeStruct((m, MIX_W), BF16),
        ),
        grid=(m // tm,),
        in_specs=[col(0), col(1), col(2), wspec, wspec],
        out_specs=(col(0), col(0), col(0), col(0)),
        compiler_params=_params(("arbitrary",)),
        name="fox_qk_norm",
    )(qkv, qkv, qkv, qw.reshape(1, HEAD_DIM), kw.reshape(1, HEAD_DIM))


def _online_softmax_update(s, v16, m_ref, l_ref, acc_ref):
    m_prev = m_ref[...]
    m_new = jnp.maximum(m_prev, jnp.max(s, axis=1, keepdims=True))
    alpha = jnp.exp(m_prev - m_new)
    p = jnp.exp(s - m_new)
    l_ref[...] = alpha * l_ref[...] + jnp.sum(p, axis=1, keepdims=True)
    acc_ref[...] = alpha * acc_ref[...] + _dot(p.astype(BF16), v16)
    m_ref[...] = m_new


FLASH_HEADS = 2


def _flash_kernel(qi_ref, kj_ref, q_ref, k_ref, v_ref, cq_ref, ck_ref, o_ref, m_ref, l_ref, acc_ref, *, scale):
    i = qi_ref[pl.program_id(2)]
    j = kj_ref[pl.program_id(2)]

    @pl.when(j == 0)
    def _():
        m_ref[...] = jnp.full_like(m_ref, NEG_BIG)
        l_ref[...] = jnp.zeros_like(l_ref)
        acc_ref[...] = jnp.zeros_like(acc_ref)

    heads = range(m_ref.shape[0])
    cols = [slice(g * HEAD_DIM, (g + 1) * HEAD_DIM) for g in heads]

    def update(diagonal):
        ss = [_dot_nt(q_ref[:, cols[g]], k_ref[:, cols[g]]) * scale + (cq_ref[g] - ck_ref[g]) for g in heads]
        if diagonal:
            keep = _iota2(ss[0].shape, 0) >= _iota2(ss[0].shape, 1)
            ss = [jnp.where(keep, s, NEG_BIG) for s in ss]
        m_prev = [m_ref[g] for g in heads]
        m_new = [jnp.maximum(m_prev[g], jnp.max(ss[g], axis=1, keepdims=True)) for g in heads]
        alpha = [jnp.exp(m_prev[g] - m_new[g]) for g in heads]
        ps = [jnp.exp(ss[g] - m_new[g]) for g in heads]
        ls = [alpha[g] * l_ref[g] + jnp.sum(ps[g], axis=1, keepdims=True) for g in heads]
        accs = [alpha[g] * acc_ref[g] + _dot(ps[g].astype(BF16), v_ref[:, cols[g]]) for g in heads]
        for g in heads:
            m_ref[g] = m_new[g]
            l_ref[g] = ls[g]
            acc_ref[g] = accs[g]

    @pl.when(j < i)
    def _():
        update(False)

    @pl.when(j == i)
    def _():
        update(True)
        for g in heads:
            o_ref[:, cols[g]] = (acc_ref[g] / l_ref[g]).astype(o_ref.dtype)


def _flash_attention(fq16, fk16, fv16, cum_col, cum_row, batch, seq, t):
    n = seq // t
    pairs = [(i, j) for i in range(n) for j in range(i + 1)]
    qi = jnp.asarray([i for i, _ in pairs], jnp.int32)
    kj = jnp.asarray([j for _, j in pairs], jnp.int32)
    g = FLASH_HEADS
    qspec = pl.BlockSpec((t, g * HEAD_DIM), lambda b, h, s, qi, kj: (b * n + qi[s], h))
    kspec = pl.BlockSpec((t, g * HEAD_DIM), lambda b, h, s, qi, kj: (b * n + kj[s], h))
    return pl.pallas_call(
        functools.partial(_flash_kernel, scale=HEAD_DIM ** -0.5),
        out_shape=jax.ShapeDtypeStruct((batch * seq, MIX_W), BF16),
        grid_spec=pltpu.PrefetchScalarGridSpec(
            num_scalar_prefetch=2,
            grid=(batch, N_HEADS // g, len(pairs)),
            in_specs=[
                qspec, kspec, kspec,
                pl.BlockSpec((None, g, t, 1), lambda b, h, s, qi, kj: (b, h, qi[s], 0)),
                pl.BlockSpec((None, g, 1, t), lambda b, h, s, qi, kj: (b, h, 0, kj[s])),
            ],
            out_specs=qspec,
            scratch_shapes=[pltpu.VMEM((g, t, 1), F32), pltpu.VMEM((g, t, 1), F32),
                            pltpu.VMEM((g, t, HEAD_DIM), F32)],
        ),
        compiler_params=_params(("arbitrary", "arbitrary", "arbitrary")),
        name="fox_flash_attention",
    )(qi, kj, fq16, fk16, fv16, cum_col, cum_row)


def _suffix_kernel(pt_ref, x_ref, o_ref, xs_ref, *, n_pages):
    b = pl.program_id(0)
    for p in range(n_pages):
        xs_ref[p * N_HEADS:(p + 1) * N_HEADS, :] = x_ref[pt_ref[b * n_pages + p]]
    x = xs_ref[...]
    rows, t = x.shape
    later = jnp.where(_iota2((t, t), 0) > _iota2((t, t), 1), 1.0, 0.0).astype(BF16)
    r = _iota2((rows, rows), 0)
    col = _iota2((rows, rows), 1)
    later_pages = jnp.where(col // N_HEADS > r // N_HEADS,
                            jnp.where(col % N_HEADS == r % N_HEADS, 1.0, 0.0), 0.0).astype(BF16)
    totals = jnp.broadcast_to(jnp.sum(x, axis=1, keepdims=True), x.shape)
    o_ref[...] = _dot_exact_rhs(x, later) + _dot_exact_lhs(later_pages, totals)


def _suffix_sums(page_table_flat, logf_t, layer, n_batch, n_pages):
    n_pool = logf_t.shape[1]
    rows = n_pages * N_HEADS
    return pl.pallas_call(
        functools.partial(_suffix_kernel, n_pages=n_pages),
        out_shape=jax.ShapeDtypeStruct((n_batch, rows, PAGE_SIZE), F32),
        grid_spec=pltpu.PrefetchScalarGridSpec(
            num_scalar_prefetch=1,
            grid=(n_batch,),
            in_specs=[pl.BlockSpec((None, n_pool, N_HEADS, PAGE_SIZE), lambda b, pt: (layer, 0, 0, 0))],
            out_specs=pl.BlockSpec((None, rows, PAGE_SIZE), lambda b, pt: (b, 0, 0)),
            scratch_shapes=[pltpu.VMEM((rows, PAGE_SIZE), F32)],
        ),
        compiler_params=_params(("arbitrary",)),
        name="fox_past_suffix",
    )(page_table_flat, logf_t)


DECODE_PAGES = 8


def _decode_kernel(pt_ref, q_ref, *refs, scale, npg):
    k_refs, v_refs = refs[:npg], refs[npg:2 * npg]
    bias_ref, cq_ref, kn_ref, vn_ref, ckn_ref, o_ref, m_ref, l_ref, acc_ref = refs[2 * npg:]
    p = pl.program_id(1)

    @pl.when(p == 0)
    def _():
        m_ref[...] = jnp.full_like(m_ref, NEG_BIG)
        l_ref[...] = jnp.zeros_like(l_ref)
        acc_ref[...] = jnp.zeros_like(acc_ref)

    def head_rows(h):
        return slice(h * SUBLANES, (h + 1) * SUBLANES)

    def head_of_page(ref, h):
        return ref[pl.ds(h, PAGE_SIZE, stride=N_HEADS), :].astype(BF16)

    heads = range(N_HEADS)
    q16s = [q_ref[head_rows(h), :].astype(BF16) for h in heads]
    cqs = [cq_ref[head_rows(h), :] for h in heads]
    bias = bias_ref[...]
    ss = [jnp.concatenate(
        [_dot_nt(q16s[h], head_of_page(k_refs[i], h)) * scale
         + (cqs[h] + bias[i * N_HEADS + h:i * N_HEADS + h + 1, :]) for i in range(npg)], axis=1) for h in heads]
    m_prev = m_ref[...]
    m_new = jnp.maximum(m_prev, jnp.concatenate([jnp.max(s, axis=1, keepdims=True) for s in ss], axis=0))
    alpha = jnp.exp(m_prev - m_new)
    ps = [jnp.exp(ss[h] - m_new[head_rows(h), :]) for h in heads]
    l_ref[...] = alpha * l_ref[...] + jnp.concatenate([jnp.sum(p, axis=1, keepdims=True) for p in ps], axis=0)
    pv = jnp.concatenate(
        [_dot(ps[h].astype(BF16), jnp.concatenate([head_of_page(v_refs[i], h) for i in range(npg)], axis=0))
         for h in heads], axis=0)
    acc_ref[...] = alpha * acc_ref[...] + pv
    m_ref[...] = m_new

    @pl.when(p == pl.num_programs(1) - 1)
    def _():
        news = [slice(h * LANES, (h + 1) * LANES) for h in heads]
        ckn = ckn_ref[...]
        s2 = [_dot_nt(q16s[h], kn_ref[news[h], :]) * scale + (cqs[h] - ckn[h:h + 1, :]) for h in heads]
        keep = _iota2(s2[0].shape, 1) <= _iota2(s2[0].shape, 0)
        s2 = [jnp.where(keep, s, NEG_BIG) for s in s2]
        m_last = jnp.maximum(m_new, jnp.concatenate([jnp.max(s, axis=1, keepdims=True) for s in s2], axis=0))
        alpha2 = jnp.exp(m_new - m_last)
        p2 = [jnp.exp(s2[h] - m_last[head_rows(h), :]) for h in heads]
        l_last = alpha2 * l_ref[...] + jnp.concatenate([jnp.sum(x, axis=1, keepdims=True) for x in p2], axis=0)
        pv2 = jnp.concatenate([_dot(p2[h].astype(BF16), vn_ref[news[h], :]) for h in heads], axis=0)
        o_ref[...] = (alpha2 * acc_ref[...] + pv2) / l_last


def _decode_attention(page_table_flat, q64, cache_k, cache_v, bias, cq, kn, vn, ckn, layer, n_batch, n_pages):
    rows = N_HEADS * SUBLANES
    flat = PAGE_SIZE * N_HEADS
    npg = DECODE_PAGES
    page = lambda i: pl.BlockSpec((None, None, flat, HEAD_DIM),
                                  lambda b, p, pt: (layer, pt[b * n_pages + p * npg + i], 0, 0))
    per_b = lambda shape: pl.BlockSpec((None,) + shape, lambda b, p, pt: (b, 0, 0))
    return pl.pallas_call(
        functools.partial(_decode_kernel, scale=HEAD_DIM ** -0.5, npg=npg),
        out_shape=jax.ShapeDtypeStruct((n_batch, rows, HEAD_DIM), F32),
        grid_spec=pltpu.PrefetchScalarGridSpec(
            num_scalar_prefetch=1,
            grid=(n_batch, n_pages // npg),
            in_specs=[per_b((rows, HEAD_DIM))] + [page(i) for i in range(npg)] + [page(i) for i in range(npg)] + [
                pl.BlockSpec((None, npg * N_HEADS, PAGE_SIZE), lambda b, p, pt: (b, p, 0)),
                per_b((rows, 1)),
                per_b((N_HEADS * LANES, HEAD_DIM)), per_b((N_HEADS * LANES, HEAD_DIM)), per_b((N_HEADS, LANES)),
            ],
            out_specs=per_b((rows, HEAD_DIM)),
            scratch_shapes=[pltpu.VMEM((rows, 1), F32), pltpu.VMEM((rows, 1), F32),
                            pltpu.VMEM((rows, HEAD_DIM), F32)],
        ),
        compiler_params=_params(("arbitrary", "arbitrary")),
        name="fox_paged_decode",
    )(page_table_flat, q64, *([cache_k] * npg), *([cache_v] * npg), bias, cq, kn, vn, ckn)


def _unit_lower_inverses(ms, c):
    r = _iota2((c, c), 0)
    col = _iota2((c, c), 1)
    eye = jnp.where(r == col, 1.0, 0.0)
    pair = r // 2 == col // 2
    invs = [eye - jnp.where(pair, m, 0.0) for m in ms]
    s = 2
    while s < c:
        same_block = r // (2 * s) == col // (2 * s)
        other_half = r // s != col // s
        offs = [jnp.where(same_block, jnp.where(other_half, m, 0.0), 0.0).astype(BF16) for m in ms]
        inv16s = [inv.astype(BF16) for inv in invs]
        mids = [_dot(off, inv16).astype(BF16) for off, inv16 in zip(offs, inv16s)]
        invs = [inv - _dot(inv16, mid) for inv, inv16, mid in zip(invs, inv16s, mids)]
        s *= 2
    return invs


def _gdn_kernel(qkv_ref, convw_ref, conv0_ref, gates_ref, gates_t_ref, z_ref, nw_ref, s0_ref,
                o_ref, st_ref, ext_ref, s_ref, *, c, rows):
    n = pl.program_id(1)

    @pl.when(n == 0)
    def _():
        ext_ref[:, 0:SUBLANES, :] = conv0_ref[...]
        s_ref[...] = s0_ref[...]

    ext_ref[:, SUBLANES:SUBLANES + c, :] = qkv_ref[...]

    def conv(b, c0):
        acc = ext_ref[b, SUBLANES:SUBLANES + c, c0:c0 + HEAD_DIM] * convw_ref[CONV_W - 1:CONV_W, c0:c0 + HEAD_DIM]
        for i in range(1, CONV_W):
            acc = acc + (ext_ref[b, SUBLANES - i:SUBLANES - i + c, c0:c0 + HEAD_DIM]
                         * convw_ref[CONV_W - 1 - i:CONV_W - i, c0:c0 + HEAD_DIM])
        return _silu(acc)

    def l2n(x):
        return x * lax.rsqrt(jnp.sum(x * x, axis=-1, keepdims=True) + EPS)

    r = _iota2((c, c), 0)
    col = _iota2((c, c), 1)
    incl_lower = jnp.where(col <= r, 1.0, 0.0).astype(BF16)
    incl_upper = jnp.where(r <= col, 1.0, 0.0).astype(BF16)
    gates_b = [gates_ref[b] for b in range(rows)]
    gc_cols = [_dot_exact_lhs(incl_lower, g) for g in gates_b]
    gc_rows = [_dot_exact_rhs(gates_t_ref[b], incl_upper) for b in range(rows)]

    chains = [(b, h) for b in range(rows) for h in range(N_HEADS)]
    heads = range(len(chains))
    qs = [l2n(conv(b, h * HEAD_DIM)) * (HEAD_DIM ** -0.5) for b, h in chains]
    ks = [l2n(conv(b, MIX_W + h * HEAD_DIM)) for b, h in chains]
    vs = [conv(b, 2 * MIX_W + h * HEAD_DIM) for b, h in chains]
    ext_ref[:, 0:SUBLANES, :] = ext_ref[:, c:c + SUBLANES, :]
    states = [s_ref[b, h] for b, h in chains]
    betas = [gates_b[b][:, 16 + h:17 + h] for b, h in chains]
    gcs = [gc_cols[b][:, 8 + h:9 + h] for b, h in chains]
    gls = [gc[c - 1:c, :] for gc in gcs]
    decays = [jnp.exp(jnp.where(col <= r, gcs[i] - gc_rows[b][8 + h:9 + h, :], NEG_BIG))
              for i, (b, h) in enumerate(chains)]
    kbs = [ks[h] * betas[h] for h in heads]
    k16s = [k.astype(BF16) for k in ks]
    ms = [jnp.where(col < r, _dot_nt(kbs[h].astype(BF16), k16s[h]) * decays[h], 0.0) for h in heads]
    a16s = [(_dot_nt(qs[h].astype(BF16), k16s[h]) * decays[h]).astype(BF16) for h in heads]
    t16s = [t.astype(BF16) for t in _unit_lower_inverses(ms, c)]
    u0s = [_dot(t16s[h], (vs[h] * betas[h]).astype(BF16)) for h in heads]
    wk16s = [_dot(t16s[h], (kbs[h] * jnp.exp(gcs[h])).astype(BF16)).astype(BF16) for h in heads]
    s16s = [s.astype(BF16) for s in states]
    u16s = [(u0s[h] - _dot(wk16s[h], s16s[h])).astype(BF16) for h in heads]
    outs = [_dot((qs[h] * jnp.exp(gcs[h])).astype(BF16), s16s[h]) + _dot(a16s[h], u16s[h]) for h in heads]
    new_states = [states[h] * jnp.exp(gls[h]) + _dot_tn((ks[h] * jnp.exp(gls[h] - gcs[h])).astype(BF16), u16s[h])
                  for h in heads]
    for i, (b, h) in enumerate(chains):
        cols = slice(h * HEAD_DIM, (h + 1) * HEAD_DIM)
        s_ref[b, h] = new_states[i]
        o_ref[b, :, cols] = (_rms(outs[i]) * nw_ref[...] * _silu(z_ref[b, :, cols])).astype(o_ref.dtype)

    @pl.when(n == pl.num_programs(1) - 1)
    def _():
        st_ref[...] = s_ref[...]


GDN_ROWS = 2


def _gdn(qkv, conv_w, conv0, gates, gates_t, z, norm_w, s0, batch, n_chunks, c, out_dtype):
    m = qkv.shape[0]
    w3 = 3 * MIX_W
    seq = n_chunks * c
    rows = GDN_ROWS
    row = lambda width: pl.BlockSpec((rows, c, width), lambda b, n: (b, n, 0))
    state = pl.BlockSpec((rows, N_HEADS, HEAD_DIM, HEAD_DIM), lambda b, n: (b, 0, 0, 0))
    by_row = lambda a: a.reshape(batch, seq, a.shape[-1])
    o, st = pl.pallas_call(
        functools.partial(_gdn_kernel, c=c, rows=rows),
        out_shape=(jax.ShapeDtypeStruct((batch, seq, MIX_W), out_dtype),
                   jax.ShapeDtypeStruct((batch, N_HEADS, HEAD_DIM, HEAD_DIM), F32)),
        grid=(batch // rows, n_chunks),
        in_specs=[
            row(w3),
            pl.BlockSpec((CONV_W, w3), lambda b, n: (0, 0)),
            pl.BlockSpec((rows, SUBLANES, w3), lambda b, n: (b, 0, 0)),
            row(LANES),
            pl.BlockSpec((rows, None, LANES, c), lambda b, n: (b, n, 0, 0)),
            row(MIX_W),
            pl.BlockSpec((1, HEAD_DIM), lambda b, n: (0, 0)),
            state,
        ],
        out_specs=(row(MIX_W), state),
        scratch_shapes=[pltpu.VMEM((rows, c + SUBLANES, w3), F32),
                        pltpu.VMEM((rows, N_HEADS, HEAD_DIM, HEAD_DIM), F32)],
        compiler_params=_params(("arbitrary", "arbitrary")),
        name="gated_deltanet",
    )(by_row(qkv), conv_w, conv0, by_row(gates), gates_t.reshape(batch, n_chunks, LANES, c), by_row(z),
      norm_w.reshape(1, HEAD_DIM), s0)
    return o.reshape(m, MIX_W), st


def _ret_kernel(q_ref, k_ref, v_ref, g_ref, cos_ref, sin_ref, s0_ref, o_ref, st_ref, s_ref, *, c, n_valid):
    n = pl.program_id(1)

    @pl.when(n == 0)
    def _():
        s_ref[...] = s0_ref[...]

    cos = cos_ref[...]
    sin = sin_ref[...]

    def rot(x):
        return x * cos + pltpu.roll(x, HEAD_DIM // 2, 1) * sin

    r = _iota2((c, c), 0)
    col = _iota2((c, c), 1)
    cnt_r = jnp.minimum(r + 1, n_valid).astype(F32)
    cnt_c = jnp.minimum(col + 1, n_valid).astype(F32)
    cnt = cnt_r[:, 0:1]
    total = float(min(c, n_valid))

    chains = [(b, h) for b in range(s_ref.shape[0]) for h in range(N_HEADS)]
    heads = range(len(chains))
    lgs = [math.log(1.0 - 2.0 ** (-5.0 - h)) for _, h in chains]
    sls = [slice(h * HEAD_DIM, (h + 1) * HEAD_DIM) for _, h in chains]
    qs = [rot(q_ref[b, :, sls[i]]) for i, (b, _) in enumerate(chains)]
    ks = [rot(k_ref[b, :, sls[i]]) * (HEAD_DIM ** -0.5) for i, (b, _) in enumerate(chains)]
    v16s = [v_ref[b, :, sls[i]].astype(BF16) for i, (b, _) in enumerate(chains)]
    states = [s_ref[b, h] for b, h in chains]
    a16s = [(_dot_nt(qs[h].astype(BF16), ks[h].astype(BF16))
             * jnp.exp(jnp.where(col <= r, (cnt_r - cnt_c) * lgs[h], NEG_BIG))).astype(BF16) for h in heads]
    outs = [_dot(a16s[h], v16s[h]) + _dot((qs[h] * jnp.exp(cnt * lgs[h])).astype(BF16), states[h].astype(BF16))
            for h in heads]
    new_states = [states[h] * math.exp(total * lgs[h])
                  + _dot_tn((ks[h] * jnp.exp((total - cnt) * lgs[h])).astype(BF16), v16s[h]) for h in heads]
    for i, (b, h) in enumerate(chains):
        s_ref[b, h] = new_states[i]
        o_ref[b, :, sls[i]] = (_rms(outs[i]) * _silu(g_ref[b, :, sls[i]])).astype(o_ref.dtype)

    @pl.when(n == pl.num_programs(1) - 1)
    def _():
        st_ref[...] = s_ref[...]


def _retention(qkvg, cos, sin, s0, batch, n_chunks, c, n_valid, out_dtype):
    m = qkvg.shape[0]
    seq = n_chunks * c
    rows = GDN_ROWS
    colblk = lambda j: pl.BlockSpec((rows, c, MIX_W), lambda b, n: (b, n, j))
    table = pl.BlockSpec((c, HEAD_DIM), lambda b, n: (n, 0))
    state = pl.BlockSpec((rows, N_HEADS, HEAD_DIM, HEAD_DIM), lambda b, n: (b, 0, 0, 0))
    x = qkvg.reshape(batch, seq, 4 * MIX_W)
    o, st = pl.pallas_call(
        functools.partial(_ret_kernel, c=c, n_valid=n_valid),
        out_shape=(jax.ShapeDtypeStruct((batch, seq, MIX_W), out_dtype),
                   jax.ShapeDtypeStruct((batch, N_HEADS, HEAD_DIM, HEAD_DIM), F32)),
        grid=(batch // rows, n_chunks),
        in_specs=[colblk(0), colblk(1), colblk(2), colblk(3), table, table, state],
        out_specs=(colblk(0), state),
        scratch_shapes=[pltpu.VMEM((rows, N_HEADS, HEAD_DIM, HEAD_DIM), F32)],
        compiler_params=_params(("arbitrary", "arbitrary")),
        name="retention",
    )(x, x, x, x, cos, sin, s0)
    return o.reshape(m, MIX_W), st


def _merge_kernel(fox_ref, gdn_ref, ret_ref, w_ref, g0_ref, g1_ref, g2_ref, o_ref, w16_ref):
    @pl.when(pl.program_id(1) == 0)
    def _():
        w16_ref[...] = w_ref[...].astype(BF16)

    acc = _sigmoid(g0_ref[...]) * _dot(fox_ref[...], w16_ref[0])
    acc = acc + _sigmoid(g1_ref[...]) * _dot(gdn_ref[...], w16_ref[1])
    acc = acc + _sigmoid(g2_ref[...]) * _dot(ret_ref[...], w16_ref[2])
    o_ref[...] = acc.astype(o_ref.dtype)


def _merge(o_fox, o_gdn, o_ret, w_branch, layer, gate, tm, tn):
    m = o_fox.shape[0]
    d = w_branch.shape[3]
    nj = d // tn
    br = pl.BlockSpec((tm, MIX_W), lambda j, i: (i, 0))
    gspec = lambda b: pl.BlockSpec((tm, tn), lambda j, i: (i, b * nj + j))
    return pl.pallas_call(
        _merge_kernel,
        out_shape=jax.ShapeDtypeStruct((m, d), BF16),
        grid=(nj, m // tm),
        in_specs=[br, br, br,
                  pl.BlockSpec((None, N_BRANCH, MIX_W, tn), lambda j, i: (layer, 0, 0, j)),
                  gspec(0), gspec(1), gspec(2)],
        out_specs=pl.BlockSpec((tm, tn), lambda j, i: (i, j)),
        scratch_shapes=[pltpu.VMEM((N_BRANCH, MIX_W, tn), BF16)],
        compiler_params=_params(("arbitrary", "arbitrary")),
        name="branch_merge",
    )(o_fox, o_gdn, o_ret, w_branch, gate, gate, gate)


def _router_kernel(h_ref, whi_ref, wlo_ref, b_ref, idx_ref, wt_ref, cnt_ref, carry_ref):
    @pl.when(pl.program_id(0) == 0)
    def _():
        carry_ref[...] = jnp.zeros_like(carry_ref)

    h_hi, h_lo = _split2(h_ref[...])
    logits = _dot(h_hi, whi_ref[...]) + (_dot(h_hi, wlo_ref[...]) + _dot(h_lo, whi_ref[...])) + b_ref[...]
    tm = logits.shape[0]
    lane = _iota2(logits.shape, 1)
    lane_f = lane.astype(F32)
    logits = jnp.where(lane < N_EXPERTS, logits, -jnp.inf)
    vals, sels = [], []
    for k in range(TOP_K):
        top = jnp.max(logits, axis=1, keepdims=True)
        sel = jnp.min(jnp.where(logits == top, lane_f, float(LANES)), axis=1, keepdims=True)
        vals.append(top)
        sels.append(sel)
        logits = jnp.where(lane_f == sel, -jnp.inf, logits)
    chosen = [jnp.where(lane_f == sel, 1.0, 0.0) for sel in sels]
    picks = chosen[0] + chosen[1] + chosen[2] + chosen[3]
    earlier_rows = jnp.where(_iota2((tm, tm), 1) < _iota2((tm, tm), 0), 1.0, 0.0).astype(BF16)
    before = _dot(earlier_rows, picks.astype(BF16)) + carry_ref[0:1, :]
    carry_ref[0:1, :] = carry_ref[0:1, :] + jnp.sum(picks, axis=0, keepdims=True)
    es = [jnp.exp(v - vals[0]) for v in vals]
    tot = es[0] + es[1] + es[2] + es[3]
    idx = jnp.zeros(logits.shape, F32)
    wt = jnp.zeros(logits.shape, F32)
    for k in range(TOP_K):
        rank = jnp.sum(before * chosen[k], axis=1, keepdims=True)
        idx = jnp.where(lane == k, sels[k], jnp.where(lane == TOP_K + k, rank, idx))
        wt = jnp.where(lane == k, es[k] / tot, wt)
    idx_ref[...] = idx.astype(jnp.int32)
    wt_ref[...] = wt
    cnt_ref[...] = carry_ref[...]


def _router(h, m, w_hi, w_lo, b, tm):
    d = h.shape[1]
    tok = pl.BlockSpec((tm, LANES), lambda i: (i, 0))
    const = lambda rows: pl.BlockSpec((rows, LANES), lambda i: (0, 0))
    return pl.pallas_call(
        _router_kernel,
        out_shape=(jax.ShapeDtypeStruct((m, LANES), jnp.int32), jax.ShapeDtypeStruct((m, LANES), F32),
                   jax.ShapeDtypeStruct((SUBLANES, LANES), F32)),
        grid=(m // tm,),
        in_specs=[pl.BlockSpec((tm, d), lambda i: (i, 0)), const(d), const(d), const(1)],
        out_specs=(tok, tok, const(SUBLANES)),
        scratch_shapes=[pltpu.VMEM((SUBLANES, LANES), F32)],
        compiler_params=_params(("arbitrary",)),
        name="moe_router_topk",
    )(h, w_hi, w_lo, b)


def _row_copy(src_hbm, src_row, dst_vmem, dst_row, sem):
    return pltpu.make_async_copy(src_hbm.at[pl.ds(src_row, 1), :], dst_vmem.at[pl.ds(dst_row, 1), :], sem)


def _gather_kernel(tok_ref, nv_ref, h_ref, o_ref, buf_ref, sems):
    b = pl.program_id(0)
    blk = buf_ref.shape[1]
    slot = b % 2

    def fetch(block, to_slot):
        def issue(g, carry):
            for u in range(SUBLANES):
                r = g * SUBLANES + u
                _row_copy(h_ref, tok_ref[block * blk + r], buf_ref.at[to_slot], r,
                          sems.at[to_slot]).start(priority=u % 2)
            return carry

        lax.fori_loop(0, blk // SUBLANES, issue, 0)

    @pl.when(b == 0)
    def _():
        fetch(0, 0)

    @pl.when(b + 1 < nv_ref[0])
    def _():
        fetch(b + 1, 1 - slot)

    @pl.when(b < nv_ref[0])
    def _():
        pltpu.make_async_copy(h_ref.at[pl.ds(0, blk), :], buf_ref.at[slot], sems.at[slot]).wait()
        o_ref[...] = buf_ref[slot].astype(o_ref.dtype)

    @pl.when(b >= nv_ref[0])
    def _():
        o_ref[...] = jnp.zeros_like(o_ref)


def _gather_rows(slot_tok, n_valid, h_ext, n_blk, blk):
    d = h_ext.shape[1]
    return pl.pallas_call(
        _gather_kernel,
        out_shape=jax.ShapeDtypeStruct((n_blk * blk, d), BF16),
        grid_spec=pltpu.PrefetchScalarGridSpec(
            num_scalar_prefetch=2,
            grid=(n_blk,),
            in_specs=[pl.BlockSpec(memory_space=pl.ANY)],
            out_specs=pl.BlockSpec((blk, d), lambda b, tok, nv: (b, 0)),
            scratch_shapes=[pltpu.VMEM((2, blk, d), F32), pltpu.SemaphoreType.DMA((2,))],
        ),
        compiler_params=_params(("arbitrary",)),
        name="moe_token_gather",
    )(slot_tok, n_valid, h_ext)


def _stream_expert_weights(be_ref, first_ref, nxt_ref, copies, stage_ref, w16_ref):
    j = pl.program_id(0)
    b = pl.program_id(1)

    @pl.when(first_ref[b] == 1)
    def _():
        @pl.when(jnp.logical_and(j == 0, b == 0))
        def _():
            for c in copies(be_ref[0], 0):
                c.start()

        for c in copies(be_ref[b], j):
            c.wait()
        w16_ref[...] = stage_ref[...].astype(BF16)
        nxt = nxt_ref[b]

        @pl.when(nxt >= 0)
        def _():
            for c in copies(nxt, j):
                c.start()

        @pl.when(jnp.logical_and(nxt < 0, j + 1 < pl.num_programs(0)))
        def _():
            for c in copies(be_ref[0], j + 1):
                c.start()


def _gmm1_kernel(be_ref, nv_ref, first_ref, nxt_ref, x_ref, w_hbm, bg_ref, bl_ref, o_ref,
                 stage_ref, w16_ref, sems, *, layer, tn, d_ff):
    b = pl.program_id(1)

    def copies(e, sweep):
        return [pltpu.make_async_copy(
            w_hbm.at[layer, e, :, pl.ds(pl.multiple_of(half * d_ff + sweep * tn, tn), tn)],
            stage_ref.at[half], sems.at[half]) for half in range(2)]

    _stream_expert_weights(be_ref, first_ref, nxt_ref, copies, stage_ref, w16_ref)

    @pl.when(b < nv_ref[0])
    def _():
        x = x_ref[...]
        glu = jnp.minimum(_dot(x, w16_ref[0]) + bg_ref[...], SWIGLU_LIMIT)
        lin = jnp.clip(_dot(x, w16_ref[1]) + bl_ref[...], -SWIGLU_LIMIT, SWIGLU_LIMIT)
        o_ref[...] = (glu * _sigmoid(SWIGLU_ALPHA * glu) * (lin + 1.0)).astype(o_ref.dtype)

    @pl.when(b >= nv_ref[0])
    def _():
        o_ref[...] = jnp.zeros_like(o_ref)


def _gmm1(sched, xs, w_e1, b_e1, layer, n_blk, blk, tn):
    depth, n_exp, d, two_f = w_e1.shape
    d_ff = two_f // 2
    nj = d_ff // tn
    bspec = lambda off: pl.BlockSpec((None, None, 1, tn), lambda j, b, be, nv, fi, nx: (layer, be[b], 0, off + j))
    bias = b_e1.reshape(depth, n_exp, 1, two_f)
    return pl.pallas_call(
        functools.partial(_gmm1_kernel, layer=layer, tn=tn, d_ff=d_ff),
        out_shape=jax.ShapeDtypeStruct((n_blk * blk, d_ff), BF16),
        grid_spec=pltpu.PrefetchScalarGridSpec(
            num_scalar_prefetch=4,
            grid=(nj, n_blk),
            in_specs=[
                pl.BlockSpec((blk, d), lambda j, b, be, nv, fi, nx: (jnp.minimum(b, nv[0] - 1), 0)),
                pl.BlockSpec(memory_space=pl.ANY),
                bspec(0), bspec(nj),
            ],
            out_specs=pl.BlockSpec((blk, tn), lambda j, b, be, nv, fi, nx: (b, j)),
            scratch_shapes=[pltpu.VMEM((2, d, tn), F32), pltpu.VMEM((2, d, tn), BF16),
                            pltpu.SemaphoreType.DMA((2,))],
        ),
        compiler_params=_params(("arbitrary", "arbitrary")),
        name="moe_expert_up",
    )(*sched, xs, w_e1, bias, bias)


def _gmm2_kernel(be_ref, nv_ref, first_ref, nxt_ref, a_ref, w_hbm, b_ref, o_ref, stage_ref, w16_ref, sems,
                 *, layer, tn):
    b = pl.program_id(1)

    def copies(e, sweep):
        return [pltpu.make_async_copy(w_hbm.at[layer, e, :, pl.ds(pl.multiple_of(sweep * tn, tn), tn)],
                                      stage_ref, sems.at[0])]

    _stream_expert_weights(be_ref, first_ref, nxt_ref, copies, stage_ref, w16_ref)

    @pl.when(b < nv_ref[0])
    def _():
        o_ref[...] = _dot(a_ref[...], w16_ref[...]) + b_ref[...]

    @pl.when(b >= nv_ref[0])
    def _():
        o_ref[...] = jnp.zeros_like(o_ref)


def _gmm2(sched, act, w_e2, b_e2, layer, n_blk, blk, tn):
    depth, n_exp, d_ff, d = w_e2.shape
    return pl.pallas_call(
        functools.partial(_gmm2_kernel, layer=layer, tn=tn),
        out_shape=jax.ShapeDtypeStruct((n_blk * blk, d), F32),
        grid_spec=pltpu.PrefetchScalarGridSpec(
            num_scalar_prefetch=4,
            grid=(d // tn, n_blk),
            in_specs=[
                pl.BlockSpec((blk, d_ff), lambda j, b, be, nv, fi, nx: (jnp.minimum(b, nv[0] - 1), 0)),
                pl.BlockSpec(memory_space=pl.ANY),
                pl.BlockSpec((None, None, 1, tn), lambda j, b, be, nv, fi, nx: (layer, be[b], 0, j)),
            ],
            out_specs=pl.BlockSpec((blk, tn), lambda j, b, be, nv, fi, nx: (b, j)),
            scratch_shapes=[pltpu.VMEM((d_ff, tn), F32), pltpu.VMEM((d_ff, tn), BF16),
                            pltpu.SemaphoreType.DMA((1,))],
        ),
        compiler_params=_params(("arbitrary", "arbitrary")),
        name="moe_expert_down",
    )(*sched, act, w_e2, b_e2.reshape(depth, n_exp, 1, d))


def _combine_kernel(pos_ref, ys_ref, wt_ref, x_ref, gate_ref, o_ref, buf_ref, sems):
    tm = x_ref.shape[0]
    i = pl.program_id(0)
    slot = i % 2

    def fetch(tile, to_slot):
        base = tile * tm * TOP_K

        def issue(t, carry):
            for k in range(TOP_K):
                _row_copy(ys_ref, pos_ref[base + t * TOP_K + k], buf_ref.at[to_slot, k], t,
                          sems.at[to_slot]).start(priority=k % 2)
            return carry

        lax.fori_loop(0, tm, issue, 0, unroll=2)

    @pl.when(i == 0)
    def _():
        fetch(0, 0)

    @pl.when(i + 1 < pl.num_programs(0))
    def _():
        fetch(i + 1, 1 - slot)

    for k in range(TOP_K):
        pltpu.make_async_copy(ys_ref.at[pl.ds(0, tm), :], buf_ref.at[slot, k], sems.at[slot]).wait()
    wt = wt_ref[...]
    acc = buf_ref[slot, 0] * wt[:, 0:1]
    for k in range(1, TOP_K):
        acc = acc + buf_ref[slot, k] * wt[:, k:k + 1]
    o_ref[...] = x_ref[...] + gate_ref[...] * acc


def _combine(pos, ys, wt, x, gate, tm, rows_per_group):
    m, d = x.shape
    r = gate.shape[1]
    tiles = rows_per_group // tm
    return pl.pallas_call(
        _combine_kernel,
        out_shape=jax.ShapeDtypeStruct((m, d), F32),
        grid_spec=pltpu.PrefetchScalarGridSpec(
            num_scalar_prefetch=1,
            grid=(m // tm,),
            in_specs=[
                pl.BlockSpec(memory_space=pl.ANY),
                pl.BlockSpec((tm, LANES), lambda i, pos: (i, 0)),
                pl.BlockSpec((tm, d), lambda i, pos: (i, 0)),
                pl.BlockSpec((None, r, d), lambda i, pos: (i // tiles, 0, 0)),
            ],
            out_specs=pl.BlockSpec((tm, d), lambda i, pos: (i, 0)),
            scratch_shapes=[pltpu.VMEM((2, TOP_K, tm, d), F32), pltpu.SemaphoreType.DMA((2,))],
        ),
        compiler_params=_params(("arbitrary",)),
        name="moe_combine",
    )(pos, ys, wt, x, gate)


TM = 512
TN = 1024


def _pad_lanes(a):
    return jnp.pad(a, ((0, 0), (0, LANES - a.shape[1])))


def _layer_weights(l, d, w_in, fox_f_bias, gdn_a_log, gdn_dt_bias, w_router, b_router):
    w = w_in[l]
    o_f = 3 * MIX_W
    o_gqkv = o_f + N_HEADS
    o_ga = o_gqkv + 3 * MIX_W
    o_gb = o_ga + N_HEADS
    o_gz = o_gb + N_HEADS
    o_ret = o_gz + MIX_W
    o_gate = o_ret + 4 * MIX_W
    small = _pad_lanes(jnp.concatenate([w[:, o_f:o_gqkv], w[:, o_ga:o_gb], w[:, o_gb:o_gz]], axis=1))
    ws_hi, ws_lo = _split2(small)
    prm = jnp.zeros((SUBLANES, LANES), F32)
    prm = prm.at[0, 0:8].set(fox_f_bias[l]).at[0, 8:16].set(gdn_dt_bias[l])
    prm = prm.at[1, 8:16].set(-jnp.exp(gdn_a_log[l]))
    wr_hi, wr_lo = _split2(_pad_lanes(w_router[l]))
    return dict(
        w_fox=w[:, 0:o_f].astype(BF16),
        w_gqkv=w[:, o_gqkv:o_ga].astype(BF16),
        w_gz=w[:, o_gz:o_ret].astype(BF16),
        w_ret=w[:, o_ret:o_gate].astype(BF16),
        w_gate=w[:, o_gate:o_gate + N_BRANCH * d].astype(BF16),
        ws_hi=ws_hi, ws_lo=ws_lo, prm=prm, wr_hi=wr_hi, wr_lo=wr_lo,
        b_router=_pad_lanes(b_router[l].reshape(1, N_EXPERTS)),
    )


def _rope_tables(pos):
    half = HEAD_DIM // 2
    inv = ROPE_BASE ** (-jnp.arange(half, dtype=F32) / half)
    ang = pos.astype(F32)[:, None] * inv[None, :]
    cos, sin = jnp.cos(ang), jnp.sin(ang)
    return jnp.concatenate([cos, cos], axis=1), jnp.concatenate([-sin, sin], axis=1)


def _project(h, lw, tm):
    fox = _mm(h, lw['w_fox'], tm, TN)
    gqkv = _mm(h, lw['w_gqkv'], tm, TN)
    gz = _mm(h, lw['w_gz'], tm, TN)
    ret = _mm(h, lw['w_ret'], tm, TN)
    gate = _mm(h, lw['w_gate'], tm, TN)
    gates = _gates(h, lw['ws_hi'], lw['ws_lo'], lw['prm'], tm)
    return fox, gqkv, gz, ret, gate, gates


def _prompt_mixer(x, mod, lw, p, batch, seq):
    m, d = x.shape
    sh1, sc1, g1, sh2, sc2, g2 = mod
    h = _norm_mod(x, p['norm1'], sc1, sh1, TM, seq)
    fox, gqkv, gz, ret, gate, gates = _project(h, lw, TM)
    cum = _cumsum_rows(gates, 256, seq)
    fq16, fk, fk16, fv16 = _fox_prep(fox, p['fox_qn'], p['fox_kn'], TM)
    cum_t = cum[:, :N_HEADS].reshape(batch, seq, N_HEADS).transpose(0, 2, 1)
    o_fox = _flash_attention(fq16, fk16, fv16, cum_t[..., None], cum_t[:, :, None, :], batch, seq, 512)
    n_chunks = seq // CHUNK
    gates_t = gates.reshape(m // CHUNK, CHUNK, LANES).transpose(0, 2, 1)
    zero_state = jnp.zeros((batch, N_HEADS, HEAD_DIM, HEAD_DIM), F32)
    o_gdn, s_gdn = _gdn(gqkv, p['gdn_conv'], jnp.zeros((batch, SUBLANES, 3 * MIX_W), F32), gates, gates_t, gz,
                        p['gdn_nw'], zero_state, batch, n_chunks, CHUNK, BF16)
    cos, sin = _rope_tables(jnp.arange(seq, dtype=jnp.int32))
    o_ret, s_ret = _retention(ret, cos, sin, zero_state, batch, n_chunks, CHUNK, CHUNK, BF16)
    merged = _merge(o_fox, o_gdn, o_ret, p['w_branch'], p['layer'], gate, TM, 512)
    x1 = _mm_residual(merged, p['w_out'], p['layer'], x, g1, TM, TN, seq)
    state = (fk.reshape(batch, seq, N_HEADS, HEAD_DIM),
             fox[:, 2 * MIX_W:].reshape(batch, seq, N_HEADS, HEAD_DIM),
             gates[:, :N_HEADS].reshape(batch, seq, N_HEADS),
             s_gdn,
             gqkv.reshape(batch, seq, 3 * MIX_W)[:, seq - (CONV_W - 1):],
             s_ret)
    return x1, state


def _sample_mixer(x, mod, lw, p, past, layer, batch, seq, past_len):
    m, d = x.shape
    sh1, sc1, g1, sh2, sc2, g2 = mod
    n_pages = past_len // PAGE_SIZE
    h = _norm_mod(x, p['norm1'], sc1, sh1, m, m)
    fox, gqkv, gz, ret, gate, gates = _project(h, lw, m)
    cum = _cumsum_rows(gates, m, seq)
    fq16, fk, fk16, fv16 = _fox_prep(fox, p['fox_qn'], p['fox_kn'], m)

    pad_q = SUBLANES - seq
    def head_major(a, pad_to):
        a = a.reshape(batch, seq, N_HEADS, HEAD_DIM).transpose(0, 2, 1, 3)
        return jnp.pad(a, ((0, 0), (0, 0), (0, pad_to - seq), (0, 0)))

    q64 = head_major(fq16, SUBLANES).reshape(batch, N_HEADS * SUBLANES, HEAD_DIM).astype(F32)
    cum_t = cum[:, :N_HEADS].reshape(batch, seq, N_HEADS).transpose(0, 2, 1)
    cq = jnp.pad(cum_t, ((0, 0), (0, 0), (0, pad_q))).reshape(batch, N_HEADS * SUBLANES, 1)
    kn = head_major(fk16, LANES).reshape(batch, N_HEADS * LANES, HEAD_DIM)
    vn = head_major(fv16, LANES).reshape(batch, N_HEADS * LANES, HEAD_DIM)
    ckn = jnp.pad(cum_t, ((0, 0), (0, 0), (0, LANES - seq)))
    bias = _suffix_sums(past['pt'], past['logf_t'], layer, batch, n_pages)
    o = _decode_attention(past['pt'], q64, past['k'], past['v'], bias, cq, kn, vn, ckn, layer, batch, n_pages)
    o_fox = (o.reshape(batch, N_HEADS, SUBLANES, HEAD_DIM)[:, :, :seq].transpose(0, 2, 1, 3)
             .reshape(m, MIX_W).astype(BF16))

    def pad_rows(a):
        return jnp.pad(a.reshape(batch, seq, -1), ((0, 0), (0, pad_q), (0, 0))).reshape(batch * SUBLANES, -1)

    def unpad_rows(a):
        return a.reshape(batch, SUBLANES, -1)[:, :seq].reshape(m, -1)

    gates_p = pad_rows(gates)
    gates_t = gates_p.reshape(batch, SUBLANES, LANES).transpose(0, 2, 1)
    conv0 = jnp.pad(past['conv'], ((0, 0), (SUBLANES - (CONV_W - 1), 0), (0, 0)))
    o_gdn, s_gdn = _gdn(pad_rows(gqkv), p['gdn_conv'], conv0, gates_p, gates_t, pad_rows(gz), p['gdn_nw'],
                        past['gdn'], batch, 1, SUBLANES, F32)
    cos, sin = _rope_tables(past_len + jnp.arange(seq, dtype=jnp.int32))
    cos = jnp.pad(cos, ((0, pad_q), (0, 0)))
    sin = jnp.pad(sin, ((0, pad_q), (0, 0)))
    o_ret, s_ret = _retention(pad_rows(ret), cos, sin, past['ret'], batch, 1, SUBLANES, seq, F32)
    o_gdn = unpad_rows(o_gdn).astype(BF16)
    o_ret = unpad_rows(o_ret).astype(BF16)

    merged = _merge(o_fox, o_gdn, o_ret, p['w_branch'], p['layer'], gate, m, 512)
    x1 = _mm_residual(merged, p['w_out'], p['layer'], x, g1, m, TN, m)
    state = (fk.reshape(batch, seq, N_HEADS, HEAD_DIM),
             fox[:, 2 * MIX_W:].reshape(batch, seq, N_HEADS, HEAD_DIM),
             gates[:, :N_HEADS].reshape(batch, seq, N_HEADS),
             s_gdn,
             gqkv.reshape(batch, seq, 3 * MIX_W)[:, seq - (CONV_W - 1):],
             s_ret)
    return x1, state


def _moe(h_ext, n, lw, w_e1, b_e1, w_e2, b_e2, layer):
    idx, wt, cnt = _router(h_ext, n, lw['wr_hi'], lw['wr_lo'], lw['b_router'], LANES)
    flat_e = idx[:, :TOP_K].reshape(-1)
    rank = idx[:, TOP_K:2 * TOP_K].reshape(-1)
    counts = cnt[0, :N_EXPERTS].astype(jnp.int32)
    n_assign = n * TOP_K
    padded = (counts + MOE_BLK - 1) // MOE_BLK * MOE_BLK
    pad_end = jnp.cumsum(padded)
    dest = ((pad_end - padded)[flat_e] + rank).astype(jnp.int32)
    n_blk = -(-n_assign // MOE_BLK) + N_EXPERTS
    slot_tok = jnp.full((n_blk * MOE_BLK,), n, jnp.int32).at[dest].set(
        jnp.arange(n_assign, dtype=jnp.int32) // TOP_K, unique_indices=True)
    blk_ids = jnp.arange(n_blk, dtype=jnp.int32)
    blk_e = jnp.sum((blk_ids[:, None] * MOE_BLK >= pad_end[None, :]).astype(jnp.int32), axis=1)
    n_valid = (pad_end[-1] // MOE_BLK).astype(jnp.int32)
    blk_e = jnp.where(blk_ids < n_valid, jnp.minimum(blk_e, N_EXPERTS - 1), 0)
    blk_e = jnp.where(blk_ids < n_valid, blk_e, jnp.max(blk_e)).astype(jnp.int32)
    prev_e = jnp.concatenate([jnp.full((1,), -1, jnp.int32), blk_e[:-1]])
    first = jnp.where(blk_ids < n_valid, blk_e != prev_e, False).astype(jnp.int32)
    experts = jnp.arange(N_EXPERTS, dtype=jnp.int32)
    later = jnp.where((experts[None, :] > blk_e[:, None]) & (counts[None, :] > 0), experts[None, :], N_EXPERTS)
    nxt = jnp.min(later, axis=1)
    nxt = jnp.where(nxt < N_EXPERTS, nxt, -1).astype(jnp.int32)
    n_valid = n_valid.reshape(1)
    sched = (blk_e, n_valid, first, nxt)
    xs = _gather_rows(slot_tok, n_valid, h_ext, n_blk, MOE_BLK)
    tn = min(MOE_TN, w_e2.shape[2], w_e2.shape[3])
    act = _gmm1(sched, xs, w_e1, b_e1, layer, n_blk, MOE_BLK, tn)
    ys = _gmm2(sched, act, w_e2, b_e2, layer, n_blk, MOE_BLK, min(2 * MOE_TN, w_e2.shape[3]))
    return ys, dest, wt


def kernel(x_prompt, x_sample, cache_fox_k, cache_fox_v, cache_fox_logf, state_gdn, state_gdn_conv, state_ret,
           page_table, c_prompt, c_sample, norm1_w, norm2_w, w_ada, b_ada, w_in, fox_f_bias, fox_qn_w, fox_kn_w,
           gdn_conv_w, gdn_a_log, gdn_dt_bias, gdn_norm_w, w_branch, w_out, w_router, b_router,
           w_e1, b_e1, w_e2, b_e2):
    batch, seq, d = x_prompt.shape
    dec_batch, dec_seq, _ = x_sample.shape
    depth = w_in.shape[0]
    n_pool = cache_fox_k.shape[1]
    n_pages = page_table.shape[1]
    past_len = n_pages * PAGE_SIZE
    mp = batch * seq
    ms = dec_batch * dec_seq

    n_cond = batch + dec_batch
    c_all = jnp.concatenate([c_prompt, c_sample, jnp.zeros((-n_cond % SUBLANES, d), F32)], axis=0)
    mod_all = _modulation(c_all, w_ada, b_ada)

    cache_k = cache_fox_k.reshape(depth, n_pool, PAGE_SIZE * N_HEADS, HEAD_DIM)
    cache_v = cache_fox_v.reshape(depth, n_pool, PAGE_SIZE * N_HEADS, HEAD_DIM)
    logf_t = cache_fox_logf.transpose(0, 1, 3, 2)
    pt_flat = page_table.reshape(-1).astype(jnp.int32)

    xp = x_prompt.reshape(mp, d)
    xs = x_sample.reshape(ms, d)
    st_p, st_s = [], []
    for l in range(depth):
        lw = _layer_weights(l, d, w_in, fox_f_bias, gdn_a_log, gdn_dt_bias, w_router, b_router)
        p = dict(norm1=norm1_w[l], norm2=norm2_w[l], fox_qn=fox_qn_w[l], fox_kn=fox_kn_w[l],
                 gdn_conv=gdn_conv_w[l], gdn_nw=gdn_norm_w[l], w_branch=w_branch, w_out=w_out, layer=l)
        mod_p = [t[:, None, :] for t in jnp.split(mod_all[l, :batch], 6, axis=-1)]
        mod_s = [t[None] for t in jnp.split(jnp.repeat(mod_all[l, batch:n_cond], dec_seq, axis=0), 6, axis=-1)]
        past = dict(pt=pt_flat, k=cache_k, v=cache_v, logf_t=logf_t, gdn=state_gdn[l],
                    conv=state_gdn_conv[l], ret=state_ret[l])
        x1p, sp = _prompt_mixer(xp, mod_p, lw, p, batch, seq)
        x1s, ss = _sample_mixer(xs, mod_s, lw, p, past, l, dec_batch, dec_seq, past_len)
        h_ext = jnp.zeros((mp + ms + SUBLANES, d), F32)
        h_ext = _norm_mod_into(h_ext, x1p, p['norm2'], mod_p[4], mod_p[3], TM, seq, 0)
        h_ext = _norm_mod_into(h_ext, x1s, p['norm2'], mod_s[4], mod_s[3], ms, ms, mp // ms)
        ys, dest, wt = _moe(h_ext, mp + ms, lw, w_e1, b_e1, w_e2, b_e2, l)
        xp = _combine(dest[:mp * TOP_K], ys, wt[:mp], x1p, mod_p[5], LANES, seq)
        xs = _combine(dest[mp * TOP_K:], ys, wt[mp:], x1s, mod_s[5], LANES, ms)
        st_p.append(sp)
        st_s.append(ss)

    def stk(sts, i):
        return jnp.stack([s[i] for s in sts])

    return (xp.reshape(batch, seq, d), xs.reshape(dec_batch, dec_seq, d),
            stk(st_p, 0), stk(st_p, 1), stk(st_p, 2), stk(st_p, 3), stk(st_p, 4), stk(st_p, 5),
            stk(st_s, 0), stk(st_s, 1), stk(st_s, 2), stk(st_s, 3), stk(st_s, 4), stk(st_s, 5))
```
